```python
import math
import jax, jax.numpy as jnp
from jax import lax
import numpy as np

D_MODEL = 1024
BATCH = 8
SEQ = 2048
DEPTH = 2

CHUNK = 64
Q_BLOCK = 128
MEM_LEN = 256

A_HEADS = 8
A_HEAD_DIM = 64
A_LEFT_CHUNKS = 8
A_BAND = (A_LEFT_CHUNKS + 1) * CHUNK
A_REL_MIN = -(CHUNK - 1)
A_REL_MAX = 256
A_REL_SIZE = A_REL_MAX - A_REL_MIN + 1

B_HEADS = 8
B_Q_LORA = 384
B_KV_LORA = 256
B_NOPE = 64
B_ROPE = 32
B_V = 64
ROPE_BASE = 10000.0

C_HEADS = 8
C_HEAD_DIM = 64

N_BRANCHES = 3
BRANCH_WIDTH = 512

A_COLS = 3 * A_HEADS * A_HEAD_DIM
B_COLS = B_Q_LORA + B_KV_LORA + B_ROPE
C_COLS = 3 * C_HEADS * C_HEAD_DIM
GATE_COLS = N_BRANCHES * D_MODEL
IN_COLS = A_COLS + B_COLS + C_COLS + C_HEADS + GATE_COLS

XA_HEADS = 4
XA_HEAD_DIM = 128

FFN_HIDDEN = ((8 * D_MODEL // 3 + 255) // 256) * 256

LN_EPS = 1e-5
RMS_EPS = 1e-6

kernel_name = "hybrid_chunk_causal_gated_encoder"


def layer_norm(x, g, b):
    xf = x.astype(jnp.float32)
    mu = jnp.mean(xf, axis=-1, keepdims=True)
    var = jnp.mean(jnp.square(xf - mu), axis=-1, keepdims=True)
    return ((xf - mu) * lax.rsqrt(var + LN_EPS) * g.astype(jnp.float32) + b.astype(jnp.float32)).astype(x.dtype)


def rms_norm(x, g):
    xf = x.astype(jnp.float32)
    ms = jnp.mean(jnp.square(xf), axis=-1, keepdims=True)
    return (xf * lax.rsqrt(ms + RMS_EPS) * g.astype(jnp.float32)).astype(x.dtype)


def rope(x, pos):
    half = x.shape[-1] // 2
    inv_freq = ROPE_BASE ** (-jnp.arange(half, dtype=jnp.float32) / half)
    ang = pos.astype(jnp.float32)[..., None] * inv_freq
    ang = ang.reshape(ang.shape[:2] + (1,) * (x.ndim - 3) + (half,))
    cos, sin = jnp.cos(ang), jnp.sin(ang)
    x1 = x[..., :half].astype(jnp.float32)
    x2 = x[..., half:].astype(jnp.float32)
    return jnp.concatenate([x1 * cos - x2 * sin, x2 * cos + x1 * sin], axis=-1).astype(x.dtype)


def split_cols(h, widths):
    out, off = [], 0
    for w in widths:
        out.append(h[..., off:off + w])
        off += w
    return out


def chunked_relpos_attention(q, k, v, rel_bias):
    B, S, H, Dh = q.shape
    nc = S // CHUNK
    pad = A_LEFT_CHUNKS * CHUNK
    kp = jnp.pad(k, ((0, 0), (pad, 0), (0, 0), (0, 0))).reshape(B, nc + A_LEFT_CHUNKS, CHUNK, H, Dh)
    vp = jnp.pad(v, ((0, 0), (pad, 0), (0, 0), (0, 0))).reshape(B, nc + A_LEFT_CHUNKS, CHUNK, H, Dh)
    k_band = jnp.concatenate([kp[:, j:j + nc] for j in range(A_LEFT_CHUNKS + 1)], axis=2)
    v_band = jnp.concatenate([vp[:, j:j + nc] for j in range(A_LEFT_CHUNKS + 1)], axis=2)
    qc = q.reshape(B, nc, CHUNK, H, Dh)
    s = jnp.einsum('bcqhd,bckhd->bchqk', qc, k_band).astype(jnp.float32) * (Dh ** -0.5)
    qi = jnp.arange(CHUNK)[:, None]
    kj = jnp.arange(A_BAND)[None, :]
    rel = pad + qi - kj
    idx = jnp.clip(rel, A_REL_MIN, A_REL_MAX) - A_REL_MIN
    bias = rel_bias.astype(jnp.float32)[:, idx]
    s = s + bias[None, None]
    chunk_ids = jnp.arange(nc)[:, None]
    valid = (chunk_ids * CHUNK - pad + kj) >= 0
    s = jnp.where(valid[None, :, None, None, :], s, -jnp.inf)
    p = jax.nn.softmax(s, axis=-1).astype(v.dtype)
    o = jnp.einsum('bchqk,bckhd->bcqhd', p, v_band)
    return o.reshape(B, S, H * Dh)


def mla_attention(c_q, c_kv, k_rope_in, pos, q_norm_g, kv_norm_g, w_uq, w_ukv):
    B, S, _ = c_q.shape
    q = (rms_norm(c_q, q_norm_g) @ w_uq).reshape(B, S, B_HEADS, B_NOPE + B_ROPE)
    q_nope = q[..., :B_NOPE]
    q_pe = rope(q[..., B_NOPE:], pos)
    kv = (rms_norm(c_kv, kv_norm_g) @ w_ukv).reshape(B, S, B_HEADS, B_NOPE + B_V)
    k_nope, v = kv[..., :B_NOPE], kv[..., B_NOPE:]
    k_pe = rope(k_rope_in, pos)
    scale = (B_NOPE + B_ROPE) ** -0.5
    key_chunk = jnp.arange(S) // CHUNK

    def block(i):
        start = i * Q_BLOCK
        qn = lax.dynamic_slice_in_dim(q_nope, start, Q_BLOCK, axis=1)
        qp = lax.dynamic_slice_in_dim(q_pe, start, Q_BLOCK, axis=1)
        s = (jnp.einsum('bqhd,bkhd->bhqk', qn, k_nope).astype(jnp.float32)
             + jnp.einsum('bqhr,bkr->bhqk', qp, k_pe).astype(jnp.float32)) * scale
        q_chunk = (start + jnp.arange(Q_BLOCK)) // CHUNK
        mask = key_chunk[None, :] <= q_chunk[:, None]
        s = jnp.where(mask, s, -jnp.inf)
        p = jax.nn.softmax(s, axis=-1).astype(v.dtype)
        return jnp.einsum('bhqk,bkhd->bqhd', p, v)

    o = lax.map(block, jnp.arange(S // Q_BLOCK))
    return o.transpose(1, 0, 2, 3, 4).reshape(B, S, B_HEADS * B_V)


def forgetting_attention(q, k, v, f_logit):
    B, S, H, Dh = q.shape
    log_f = jax.nn.log_sigmoid(f_logit.astype(jnp.float32))
    F = jnp.cumsum(log_f, axis=1).transpose(0, 2, 1)
    scale = Dh ** -0.5
    key_pos = jnp.arange(S)

    def block(i):
        start = i * Q_BLOCK
        qb = lax.dynamic_slice_in_dim(q, start, Q_BLOCK, axis=1)
        Fq = lax.dynamic_slice_in_dim(F, start, Q_BLOCK, axis=2)
        s = jnp.einsum('bqhd,bkhd->bhqk', qb, k).astype(jnp.float32) * scale
        s = s + Fq[..., :, None] - F[..., None, :]
        mask = key_pos[None, :] <= (start + jnp.arange(Q_BLOCK))[:, None]
        s = jnp.where(mask, s, -jnp.inf)
        p = jax.nn.softmax(s, axis=-1).astype(v.dtype)
        return jnp.einsum('bhqk,bkhd->bqhd', p, v)

    o = lax.map(block, jnp.arange(S // Q_BLOCK))
    return o.transpose(1, 0, 2, 3, 4).reshape(B, S, H * Dh)


def hybrid_mixer(x, pos, w_in, b_gate, b_forget, a_rel_bias, b_q_norm, b_kv_norm,
                 b_w_uq, b_w_ukv, w_branch, w_out):
    B, S, _ = x.shape
    h = x @ w_in
    a_qkv, b_cq, b_ckv, b_kr, c_qkv, c_f, g = split_cols(
        h, [A_COLS, B_Q_LORA, B_KV_LORA, B_ROPE, C_COLS, C_HEADS, GATE_COLS])
    a_qkv = a_qkv.reshape(B, S, 3, A_HEADS, A_HEAD_DIM)
    y_a = chunked_relpos_attention(a_qkv[:, :, 0], a_qkv[:, :, 1], a_qkv[:, :, 2], a_rel_bias)
    y_b = mla_attention(b_cq, b_ckv, b_kr, pos, b_q_norm, b_kv_norm, b_w_uq, b_w_ukv)
    c_qkv = c_qkv.reshape(B, S, 3, C_HEADS, C_HEAD_DIM)
    y_c = forgetting_attention(c_qkv[:, :, 0], c_qkv[:, :, 1], c_qkv[:, :, 2], c_f + b_forget)
    branches = jnp.stack([y_a, y_b, y_c], axis=2)
    proj = jnp.einsum('bsnw,nwd->bsnd', branches, w_branch)
    gates = jax.nn.sigmoid(g.reshape(B, S, N_BRANCHES, D_MODEL) + b_gate)
    merged = jnp.sum(gates * proj, axis=2)
    return merged @ w_out


def memory_cross_attention(x, mem, w_q, w_kv, w_o):
    B, S, _ = x.shape
    M = mem.shape[1]
    q = (x @ w_q).reshape(B, S, XA_HEADS, XA_HEAD_DIM)
    kv = (mem @ w_kv).reshape(B, M, 2, XA_HEADS, XA_HEAD_DIM)
    s = jnp.einsum('bqhd,bkhd->bhqk', q, kv[:, :, 0]).astype(jnp.float32) * (XA_HEAD_DIM ** -0.5)
    p = jax.nn.softmax(s, axis=-1).astype(x.dtype)
    o = jnp.einsum('bhqk,bkhd->bqhd', p, kv[:, :, 1]).reshape(B, S, XA_HEADS * XA_HEAD_DIM)
    return o @ w_o


def swiglu_ffn(x, w_gu, w_down):
    gu = x @ w_gu
    g, u = gu[..., :FFN_HIDDEN], gu[..., FFN_HIDDEN:]
    return (jax.nn.silu(g) * u) @ w_down


def setup_inputs(seed: int = 0) -> dict:
    key = jax.random.key(seed)
    ks = jax.random.split(key, 24)
    f32 = jnp.float32
    L = DEPTH
    beta = (8 * DEPTH) ** -0.25

    def w(k, shape, fan_in, scale=1.0):
        return jax.random.normal(k, shape, f32) * (scale * fan_in ** -0.5)

    def gain(k, shape):
        return 1.0 + 0.05 * jax.random.normal(k, shape, f32)

    def small(k, shape, s=0.02):
        return s * jax.random.normal(k, shape, f32)

    x = jax.random.normal(ks[0], (BATCH, SEQ, D_MODEL), f32)
    mem = jax.random.normal(ks[1], (BATCH, MEM_LEN, D_MODEL), f32)
    offs = jax.random.randint(ks[2], (BATCH, 1), 0, 16) * CHUNK
    positions = (jnp.arange(SEQ, dtype=jnp.int32)[None, :] + offs).astype(jnp.int32)

    return {
        "x": x,
        "mem": mem,
        "positions": positions,
        "ln_mix_g": gain(ks[3], (L, D_MODEL)),
        "ln_mix_b": small(ks[4], (L, D_MODEL)),
        "w_in": w(ks[5], (L, D_MODEL, IN_COLS), D_MODEL),
        "b_gate": small(ks[6], (L, N_BRANCHES, D_MODEL)),
        "b_forget": 3.0 + 0.5 * jax.random.normal(ks[7], (L, C_HEADS), f32),
        "a_rel_bias": small(ks[8], (L, A_HEADS, A_REL_SIZE), 0.5),
        "b_q_norm": gain(ks[9], (L, B_Q_LORA)),
        "b_kv_norm": gain(ks[10], (L, B_KV_LORA)),
        "b_w_uq": w(ks[11], (L, B_Q_LORA, B_HEADS * (B_NOPE + B_ROPE)), B_Q_LORA),
        "b_w_ukv": w(ks[12], (L, B_KV_LORA, B_HEADS * (B_NOPE + B_V)), B_KV_LORA),
        "w_branch": w(ks[13], (L, N_BRANCHES, BRANCH_WIDTH, D_MODEL), BRANCH_WIDTH),
        "w_mix_out": w(ks[14], (L, D_MODEL, D_MODEL), D_MODEL, beta),
        "ln_xa_g": gain(ks[15], (L, D_MODEL)),
        "ln_xa_b": small(ks[16], (L, D_MODEL)),
        "xa_w_q": w(ks[17], (L, D_MODEL, XA_HEADS * XA_HEAD_DIM), D_MODEL),
        "xa_w_kv": w(ks[18], (L, D_MODEL, 2 * XA_HEADS * XA_HEAD_DIM), D_MODEL),
        "xa_w_o": w(ks[19], (L, XA_HEADS * XA_HEAD_DIM, D_MODEL), XA_HEADS * XA_HEAD_DIM, beta),
        "ln_ffn_g": gain(ks[20], (L, D_MODEL)),
        "ln_ffn_b": small(ks[21], (L, D_MODEL)),
        "ffn_w_gu": w(ks[22], (L, D_MODEL, 2 * FFN_HIDDEN), D_MODEL),
        "ffn_w_down": w(ks[23], (L, FFN_HIDDEN, D_MODEL), FFN_HIDDEN, beta),
    }


def reference(x, mem, positions, ln_mix_g, ln_mix_b, w_in, b_gate, b_forget, a_rel_bias,
              b_q_norm, b_kv_norm, b_w_uq, b_w_ukv, w_branch, w_mix_out, ln_xa_g, ln_xa_b,
              xa_w_q, xa_w_kv, xa_w_o, ln_ffn_g, ln_ffn_b, ffn_w_gu, ffn_w_down):
    alpha = (2 * DEPTH) ** 0.25
    for l in range(DEPTH):
        y = hybrid_mixer(x, positions, w_in[l], b_gate[l], b_forget[l], a_rel_bias[l],
                         b_q_norm[l], b_kv_norm[l], b_w_uq[l], b_w_ukv[l], w_branch[l], w_mix_out[l])
        x = layer_norm(alpha * x + y, ln_mix_g[l], ln_mix_b[l])
        y = memory_cross_attention(x, mem, xa_w_q[l], xa_w_kv[l], xa_w_o[l])
        x = layer_norm(alpha * x + y, ln_xa_g[l], ln_xa_b[l])
        y = swiglu_ffn(x, ffn_w_gu[l], ffn_w_down[l])
        x = layer_norm(alpha * x + y, ln_ffn_g[l], ln_ffn_b[l])
    return x
```

```python
import functools
import math

import jax
import jax.numpy as jnp
from jax import lax
from jax.experimental import pallas as pl
from jax.experimental.pallas import tpu as pltpu

D_MODEL = 1024
BATCH = 8
SEQ = 2048
DEPTH = 2
TOKENS = BATCH * SEQ
CHUNK = 64
MEM_LEN = 256

HEADS = 8
HEAD_DIM = 64
PAIRS = HEADS // 2
PAIR_W = 2 * HEAD_DIM
BRANCH_W = HEADS * HEAD_DIM

A_LEFT_CHUNKS = 8
A_REL_MIN = -(CHUNK - 1)
A_REL_MAX = 256
A_TQ = 128
A_KBLOCKS = 5

B_Q_LORA = 384
B_KV_LORA = 256
B_NOPE = 64
B_ROPE = 32
B_PAIR_W = 256
ROPE_BASE = 10000.0

XA_HEADS = 4
XA_HEAD_DIM = 128
FFN_HIDDEN = 2816
FFN_CHUNK = 256

LN_EPS = 1e-5
RMS_EPS = 1e-6
ALPHA = (2 * DEPTH) ** 0.25
LOG2E = math.log2(math.e)

FLASH_T = 256

COL_A = 0
COL_C = 1536
COL_G = 3072
COL_CQ = 6144
COL_CKV = 6528
COL_KR = 6784
IN_COLS_PADDED = 7040
LAT_W = 896

VMEM_LIMIT = 56 * 1024 * 1024

bf16 = jnp.bfloat16
f32 = jnp.float32


def _dot(a, b):
    return jnp.dot(a, b, preferred_element_type=f32)


def _dot_t(a, b):
    return lax.dot_general(a, b, (((1,), (1,)), ((), ())), preferred_element_type=f32)


def _layer_norm(z, g, b):
    mu = jnp.mean(z, axis=-1, keepdims=True)
    zc = z - mu
    var = jnp.mean(zc * zc, axis=-1, keepdims=True)
    return zc * lax.rsqrt(var + LN_EPS) * g + b


def _rms_norm(z, g):
    ms = jnp.mean(z * z, axis=-1, keepdims=True)
    return z * lax.rsqrt(ms + RMS_EPS) * g


def _params(*sem):
    return pltpu.CompilerParams(dimension_semantics=sem, vmem_limit_bytes=VMEM_LIMIT)


def _resident(shape):
    zeros = (0,) * len(shape)
    return pl.BlockSpec(shape, lambda *_: zeros, pipeline_mode=pl.Buffered(1))


def _inproj_kernel(x_ref, w_ref, bg_ref, qkva_ref, qkvc_ref, gate_ref, lat_ref):
    xb = x_ref[...].astype(bf16)
    qk_scale = HEAD_DIM ** -0.5 * LOG2E

    def mm(c0, n):
        return _dot(xb, w_ref[:, c0:c0 + n])

    for dst, base in ((qkva_ref, COL_A), (qkvc_ref, COL_C)):
        dst[:, 0:512] = (mm(base, 512) * qk_scale).astype(bf16)
        dst[:, 512:1024] = mm(base + 512, 512).astype(bf16)
        dst[:, 1024:1536] = mm(base + 1024, 512).astype(bf16)
    for c in range(6):
        z = mm(COL_G + c * 512, 512) + bg_ref[:, c * 512:(c + 1) * 512]
        gate_ref[:, c * 512:(c + 1) * 512] = (1.0 / (1.0 + jnp.exp(-z))).astype(bf16)
    lat_ref[:, 0:384] = mm(COL_CQ, 384)
    lat_ref[:, 384:640] = mm(COL_CKV, 256)
    lat_ref[:, 640:896] = mm(COL_KR, 256)


def _inproj(x, w, bg):
    tm = 512
    return pl.pallas_call(
        _inproj_kernel,
        grid=(TOKENS // tm,),
        in_specs=[
            pl.BlockSpec((tm, D_MODEL), lambda i: (i, 0)),
            _resident((D_MODEL, IN_COLS_PADDED)),
            _resident((1, 3 * D_MODEL)),
        ],
        out_specs=[
            pl.BlockSpec((tm, 1536), lambda i: (i, 0)),
            pl.BlockSpec((tm, 1536), lambda i: (i, 0)),
            pl.BlockSpec((tm, 3 * D_MODEL), lambda i: (i, 0)),
            pl.BlockSpec((tm, LAT_W), lambda i: (i, 0)),
        ],
        out_shape=[
            jax.ShapeDtypeStruct((TOKENS, 1536), bf16),
            jax.ShapeDtypeStruct((TOKENS, 1536), bf16),
            jax.ShapeDtypeStruct((TOKENS, 3 * D_MODEL), bf16),
            jax.ShapeDtypeStruct((TOKENS, LAT_W), f32),
        ],
        compiler_params=_params("parallel"),
        name="inproj",
    )(x, w, bg)


def _latent_kernel(lat_ref, pos_ref, freq_ref, sign_ref, gq_ref, gkv_ref,
                   wq1_ref, wq2_ref, wk_ref, wv_ref, qb_ref, kb_ref, vb_ref):
    ang = pos_ref[...].astype(f32) * freq_ref[...]
    live = (freq_ref[...] > 0.0).astype(f32)
    cos_t = jnp.cos(ang) * live
    sin_t = jnp.sin(ang) * sign_ref[...]
    scale = (B_NOPE + B_ROPE) ** -0.5 * LOG2E

    cqn = _rms_norm(lat_ref[:, 0:384], gq_ref[...]).astype(bf16)
    q1 = _dot(cqn, wq1_ref[...])
    q2 = _dot(cqn, wq2_ref[...])
    ckvn = _rms_norm(lat_ref[:, 384:640], gkv_ref[...]).astype(bf16)
    kn = _dot(ckvn, wk_ref[...])
    vb_ref[...] = _dot(ckvn, wv_ref[...]).astype(bf16)
    kpe = (lat_ref[:, 640:768] * cos_t + lat_ref[:, 768:896] * sin_t).astype(bf16)
    for p in range(PAIRS):
        c0 = p * B_PAIR_W
        qb_ref[:, c0:c0 + 128] = (q1[:, c0:c0 + 128] * scale).astype(bf16)
        pe = q1[:, c0 + 128:c0 + 256] * cos_t + q2[:, p * 128:(p + 1) * 128] * sin_t
        qb_ref[:, c0 + 128:c0 + 256] = (pe * scale).astype(bf16)
        kb_ref[:, c0:c0 + 128] = kn[:, p * 128:(p + 1) * 128].astype(bf16)
        kb_ref[:, c0 + 128:c0 + 256] = kpe


def _latent(lat, pos, freq, sign, gq, gkv, wq1, wq2, wk, wv):
    tm = 512
    row = lambda i: (i, 0)
    return pl.pallas_call(
        _latent_kernel,
        grid=(TOKENS // tm,),
        in_specs=[
            pl.BlockSpec((tm, LAT_W), row),
            pl.BlockSpec((tm, 1), row),
            _resident((1, 128)),
            _resident((1, 128)),
            _resident((1, B_Q_LORA)),
            _resident((1, B_KV_LORA)),
            _resident((B_Q_LORA, PAIRS * B_PAIR_W)),
            _resident((B_Q_LORA, PAIRS * 128)),
            _resident((B_KV_LORA, BRANCH_W)),
            _resident((B_KV_LORA, BRANCH_W)),
        ],
        out_specs=[
            pl.BlockSpec((tm, PAIRS * B_PAIR_W), row),
            pl.BlockSpec((tm, PAIRS * B_PAIR_W), row),
            pl.BlockSpec((tm, BRANCH_W), row),
        ],
        out_shape=[
            jax.ShapeDtypeStruct((TOKENS, PAIRS * B_PAIR_W), bf16),
            jax.ShapeDtypeStruct((TOKENS, PAIRS * B_PAIR_W), bf16),
            jax.ShapeDtypeStruct((TOKENS, BRANCH_W), bf16),
        ],
        compiler_params=_params("parallel"),
        name="latent",
    )(lat, pos, freq, sign, gq, gkv, wq1, wq2, wk, wv)


def _forget_kernel(cf_ref, bf_ref, o_ref):
    rows = cf_ref.shape[0]
    lane = lax.broadcasted_iota(jnp.int32, (rows, 128), 1)
    carry = jnp.zeros((rows, 1), f32)
    for blk in range(SEQ // 128):
        z = cf_ref[:, blk * 128:(blk + 1) * 128] + bf_ref[...]
        acc = jnp.minimum(z, 0.0) - jnp.log(1.0 + jnp.exp(-jnp.abs(z)))
        d = 1
        while d < 128:
            acc = acc + jnp.where(lane >= d, pltpu.roll(acc, d, 1), 0.0)
            d *= 2
        acc = acc + carry
        o_ref[:, blk * 128:(blk + 1) * 128] = acc
        carry = acc[:, 127:128]


def _forget_cumsum(cf_t, bf_col):
    rows = cf_t.shape[0]
    return pl.pallas_call(
        _forget_kernel,
        out_shape=jax.ShapeDtypeStruct((rows, SEQ), f32),
        name="forget_cumsum",
    )(cf_t, bf_col)


def _pair_select(x, width):
    lane = lax.broadcasted_iota(jnp.int32, (1, width), 1)
    if width == PAIR_W:
        in_a = lane < HEAD_DIM
        in_b = lane >= HEAD_DIM
    else:
        in_a = (lane < 64) | ((lane >= 128) & (lane < 160))
        in_b = ((lane >= 64) & (lane < 128)) | ((lane >= 160) & (lane < 192))
    zero = jnp.zeros_like(x)
    return jnp.where(in_a, x, zero), jnp.where(in_b, x, zero)


def _values_with_ones(v):
    lane = lax.broadcasted_iota(jnp.int32, (1, PAIR_W), 1)
    one = jnp.ones_like(v)
    return jnp.where(lane < HEAD_DIM, v, one), jnp.where(lane < HEAD_DIM, one, v)


def _normalise_pair(acc_a, acc_b):
    lane = lax.broadcasted_iota(jnp.int32, (1, PAIR_W), 1)
    ra = acc_a / pltpu.roll(acc_a, HEAD_DIM, 1)
    rb = acc_b / pltpu.roll(acc_b, HEAD_DIM, 1)
    return jnp.where(lane < HEAD_DIM, ra, rb)


def _attn_a_kernel(q_ref, k_ref, v_ref, bias_ref, o_ref):
    i = pl.program_id(2)
    qa, qb = _pair_select(q_ref[...], PAIR_W)
    neg = jnp.float32(-jnp.inf)
    sa, sb, va, vb = [], [], [], []
    for t in range(A_KBLOCKS):
        kblk = i - (A_KBLOCKS - 1) + t
        off = pl.multiple_of(jnp.maximum(kblk, 0) * A_TQ, A_TQ)
        kt = k_ref[pl.ds(off, A_TQ), :]
        s_a = _dot_t(qa, kt) + bias_ref[0, t]
        s_b = _dot_t(qb, kt) + bias_ref[1, t]
        if t < A_KBLOCKS - 1:
            s_a = jnp.where(kblk >= 0, s_a, neg)
            s_b = jnp.where(kblk >= 0, s_b, neg)
        v_a, v_b = _values_with_ones(v_ref[pl.ds(off, A_TQ), :])
        sa.append(s_a); sb.append(s_b); va.append(v_a); vb.append(v_b)

    def finish(scores, values):
        m = functools.reduce(jnp.maximum, [jnp.max(s, axis=-1, keepdims=True) for s in scores])
        acc = None
        for s, v in zip(scores, values):
            term = _dot(jnp.exp2(s - m).astype(bf16), v)
            acc = term if acc is None else acc + term
        return acc

    o_ref[...] = _normalise_pair(finish(sa, va), finish(sb, vb)).astype(bf16)


def _attn_a(qkv, bias):
    nq = SEQ // A_TQ
    return pl.pallas_call(
        _attn_a_kernel,
        grid=(BATCH, PAIRS, nq),
        in_specs=[
            pl.BlockSpec((A_TQ, PAIR_W), lambda b, j, i: (b * nq + i, j)),
            pl.BlockSpec((SEQ, PAIR_W), lambda b, j, i: (b, PAIRS + j)),
            pl.BlockSpec((SEQ, PAIR_W), lambda b, j, i: (b, 2 * PAIRS + j)),
            pl.BlockSpec((2, A_KBLOCKS, A_TQ, A_TQ), lambda b, j, i: (j, 0, 0, 0)),
        ],
        out_specs=pl.BlockSpec((A_TQ, PAIR_W), lambda b, j, i: (b * nq + i, j)),
        out_shape=jax.ShapeDtypeStruct((TOKENS, BRANCH_W), bf16),
        compiler_params=_params("parallel", "parallel", "arbitrary"),
        name="attn_a",
    )(qkv, qkv, qkv, bias)


def _flash_kernel(*refs, mode, dq):
    if mode == "fox":
        q_ref, k_ref, v_ref, f_ref, o_ref = refs
    else:
        q_ref, k_ref, v_ref, o_ref = refs
    i = pl.program_id(2)
    qa, qb = _pair_select(q_ref[...], dq)
    neg = jnp.float32(-jnp.inf)

    def tile(t, carry, diagonal):
        m_a, m_b, acc_a, acc_b = carry
        off = pl.multiple_of(t * FLASH_T, FLASH_T)
        kt = k_ref[pl.ds(off, FLASH_T), :]
        v_a, v_b = _values_with_ones(v_ref[pl.ds(off, FLASH_T), :])
        s_a = _dot_t(qa, kt)
        s_b = _dot_t(qb, kt)
        if mode == "fox":
            s_a = s_a + (f_ref[0, i][:, 0:1] - f_ref[0, t]) * LOG2E
            s_b = s_b + (f_ref[1, i][:, 0:1] - f_ref[1, t]) * LOG2E
        if diagonal:
            row = lax.broadcasted_iota(jnp.int32, (FLASH_T, FLASH_T), 0)
            col = lax.broadcasted_iota(jnp.int32, (FLASH_T, FLASH_T), 1)
            if mode == "fox":
                keep = col <= row
            else:
                keep = (col // CHUNK) <= (row // CHUNK)
            s_a = jnp.where(keep, s_a, neg)
            s_b = jnp.where(keep, s_b, neg)

        def update(s, m, acc, v):
            m_new = jnp.maximum(m, jnp.max(s, axis=-1, keepdims=True))
            p = jnp.exp2(s - m_new).astype(bf16)
            return m_new, acc * jnp.exp2(m - m_new) + _dot(p, v)

        m_a, acc_a = update(s_a, m_a, acc_a, v_a)
        m_b, acc_b = update(s_b, m_b, acc_b, v_b)
        return m_a, m_b, acc_a, acc_b

    init = (jnp.full((FLASH_T, 1), neg, f32), jnp.full((FLASH_T, 1), neg, f32),
            jnp.zeros((FLASH_T, PAIR_W), f32), jnp.zeros((FLASH_T, PAIR_W), f32))
    carry = lax.fori_loop(0, i, lambda t, c: tile(t, c, False), init)
    _, _, acc_a, acc_b = tile(i, carry, True)
    o_ref[...] = _normalise_pair(acc_a, acc_b).astype(bf16)


def _flash(q, k, v, forget, *, mode, dq, kcol0, vcol0):
    nq = SEQ // FLASH_T
    in_specs = [
        pl.BlockSpec((FLASH_T, dq), lambda b, j, i: (b * nq + i, j)),
        pl.BlockSpec((SEQ, dq), lambda b, j, i: (b, kcol0 + j)),
        pl.BlockSpec((SEQ, PAIR_W), lambda b, j, i: (b, vcol0 + j)),
    ]
    args = [q, k, v]
    if mode == "fox":
        in_specs.append(pl.BlockSpec((None, None, 2, nq, 1, FLASH_T),
                                     lambda b, j, i: (b, j, 0, 0, 0, 0)))
        args.append(forget)
    return pl.pallas_call(
        functools.partial(_flash_kernel, mode=mode, dq=dq),
        grid=(BATCH, PAIRS, nq),
        in_specs=in_specs,
        out_specs=pl.BlockSpec((FLASH_T, PAIR_W), lambda b, j, i: (b * nq + i, j)),
        out_shape=jax.ShapeDtypeStruct((TOKENS, BRANCH_W), bf16),
        compiler_params=_params("parallel", "parallel", "arbitrary"),
        name="flash_" + mode,
    )(*args)


def _merge_kernel(x_ref, ya_ref, yb_ref, yc_ref, gate_ref, wb_ref, wo_ref, g_ref, b_ref, o_ref):
    merged = None
    for n, y_ref in enumerate((ya_ref, yb_ref, yc_ref)):
        proj = _dot(y_ref[...], wb_ref[n])
        term = gate_ref[:, n * D_MODEL:(n + 1) * D_MODEL].astype(f32) * proj
        merged = term if merged is None else merged + term
    y = _dot(merged.astype(bf16), wo_ref[...])
    o_ref[...] = _layer_norm(ALPHA * x_ref[...] + y, g_ref[...], b_ref[...])


def _merge(x, ya, yb, yc, gates, wb, wo, g, b):
    tm = 512
    row = lambda i: (i, 0)
    return pl.pallas_call(
        _merge_kernel,
        grid=(TOKENS // tm,),
        in_specs=[
            pl.BlockSpec((tm, D_MODEL), row),
            pl.BlockSpec((tm, BRANCH_W), row),
            pl.BlockSpec((tm, BRANCH_W), row),
            pl.BlockSpec((tm, BRANCH_W), row),
            pl.BlockSpec((tm, 3 * D_MODEL), row),
            _resident((3, BRANCH_W, D_MODEL)),
            _resident((D_MODEL, D_MODEL)),
            _resident((1, D_MODEL)),
            _resident((1, D_MODEL)),
        ],
        out_specs=pl.BlockSpec((tm, D_MODEL), row),
        out_shape=jax.ShapeDtypeStruct((TOKENS, D_MODEL), f32),
        compiler_params=_params("parallel"),
        name="merge",
    )(x, ya, yb, yc, gates, wb, wo, g, b)


def _xattn_kernel(x_ref, mem_ref, wq_ref, wkv_ref, wo_ref, g_ref, b_ref, o_ref, kv_ref):
    @pl.when(pl.program_id(1) == 0)
    def _():
        kv_ref[...] = _dot(mem_ref[...].astype(bf16), wkv_ref[...]).astype(bf16)

    x = x_ref[...]
    scale = XA_HEAD_DIM ** -0.5 * LOG2E
    q = (_dot(x.astype(bf16), wq_ref[...]) * scale).astype(bf16)
    width = XA_HEADS * XA_HEAD_DIM
    outs = []
    for h in range(XA_HEADS):
        c0 = h * XA_HEAD_DIM
        s = _dot_t(q[:, c0:c0 + XA_HEAD_DIM], kv_ref[:, c0:c0 + XA_HEAD_DIM])
        p = jnp.exp2(s - jnp.max(s, axis=-1, keepdims=True))
        l = jnp.sum(p, axis=-1, keepdims=True)
        o = _dot(p.astype(bf16), kv_ref[:, width + c0:width + c0 + XA_HEAD_DIM])
        outs.append((o / l).astype(bf16))
    y = _dot(jnp.concatenate(outs, axis=-1), wo_ref[...])
    o_ref[...] = _layer_norm(ALPHA * x + y, g_ref[...], b_ref[...])


def _xattn(x, mem, wq, wkv, wo, g, b):
    tm = 512
    nq = SEQ // tm
    width = XA_HEADS * XA_HEAD_DIM
    return pl.pallas_call(
        _xattn_kernel,
        grid=(BATCH, nq),
        in_specs=[
            pl.BlockSpec((tm, D_MODEL), lambda bi, i: (bi * nq + i, 0)),
            pl.BlockSpec((MEM_LEN, D_MODEL), lambda bi, i: (bi, 0)),
            _resident((D_MODEL, width)),
            _resident((D_MODEL, 2 * width)),
            _resident((width, D_MODEL)),
            _resident((1, D_MODEL)),
            _resident((1, D_MODEL)),
        ],
        out_specs=pl.BlockSpec((tm, D_MODEL), lambda bi, i: (bi * nq + i, 0)),
        out_shape=jax.ShapeDtypeStruct((TOKENS, D_MODEL), f32),
        scratch_shapes=[pltpu.VMEM((MEM_LEN, 2 * width), bf16)],
        compiler_params=_params("arbitrary", "arbitrary"),
        name="xattn",
    )(x, mem, wq, wkv, wo, g, b)


def _ffn_kernel(x_ref, wgu_ref, wd_ref, g_ref, b_ref, o_ref, acc_ref):
    x = x_ref[...]
    xb = x.astype(bf16)
    for c in range(FFN_HIDDEN // FFN_CHUNK):
        c0 = c * FFN_CHUNK
        gate = _dot(xb, wgu_ref[:, c0:c0 + FFN_CHUNK])
        up = _dot(xb, wgu_ref[:, FFN_HIDDEN + c0:FFN_HIDDEN + c0 + FFN_CHUNK])
        h = (gate / (1.0 + jnp.exp(-gate)) * up).astype(bf16)
        part = _dot(h, wd_ref[c0:c0 + FFN_CHUNK, :])
        if c == 0:
            acc_ref[...] = part
        else:
            acc_ref[...] += part
    o_ref[...] = _layer_norm(ALPHA * x + acc_ref[...], g_ref[...], b_ref[...])


def _ffn(x, wgu, wd, g, b):
    tm = 512
    row = lambda i: (i, 0)
    return pl.pallas_call(
        _ffn_kernel,
        grid=(TOKENS // tm,),
        in_specs=[
            pl.BlockSpec((tm, D_MODEL), row),
            _resident((D_MODEL, 2 * FFN_HIDDEN)),
            _resident((FFN_HIDDEN, D_MODEL)),
            _resident((1, D_MODEL)),
            _resident((1, D_MODEL)),
        ],
        out_specs=pl.BlockSpec((tm, D_MODEL), row),
        out_shape=jax.ShapeDtypeStruct((TOKENS, D_MODEL), f32),
        scratch_shapes=[pltpu.VMEM((tm, D_MODEL), f32)],
        compiler_params=_params("parallel"),
        name="ffn",
    )(x, wgu, wd, g, b)


def _arrange_w_in(w):
    a = w[:, 0:1536]
    cq = w[:, 1536:1920]
    ckv = w[:, 1920:2176]
    kr = w[:, 2176:2208]
    c = w[:, 2208:3744]
    cf = w[:, 3744:3752]
    g = w[:, 3752:6824]
    half = B_ROPE // 2
    kr_sw = jnp.concatenate([kr[:, half:], kr[:, :half]], axis=1)
    rows = w.shape[0]
    kra = jnp.concatenate([kr, kr, cf, jnp.zeros((rows, 56), w.dtype)], axis=1)
    krb = jnp.concatenate([kr_sw, kr_sw, jnp.zeros((rows, 64), w.dtype)], axis=1)
    return jnp.concatenate([a, c, g, cq, ckv, kra, krb], axis=1).astype(bf16)


def _arrange_w_uq(w):
    rows = w.shape[0]
    w = w.reshape(rows, HEADS, B_NOPE + B_ROPE)
    nope = w[:, :, :B_NOPE]
    pe = w[:, :, B_NOPE:]
    half = B_ROPE // 2
    pe_sw = jnp.concatenate([pe[:, :, half:], pe[:, :, :half]], axis=2)
    z64 = jnp.zeros((rows, 64), w.dtype)
    main, swapped = [], []
    for p in range(PAIRS):
        a, b = 2 * p, 2 * p + 1
        main += [nope[:, a], nope[:, b], pe[:, a], pe[:, b], z64]
        swapped += [pe_sw[:, a], pe_sw[:, b], z64]
    return (jnp.concatenate(main, axis=1).astype(bf16),
            jnp.concatenate(swapped, axis=1).astype(bf16))


def _arrange_w_ukv(w):
    rows = w.shape[0]
    w = w.reshape(rows, HEADS, B_NOPE + HEAD_DIM)
    wk = w[:, :, :B_NOPE].reshape(rows, BRANCH_W)
    wv = w[:, :, B_NOPE:].reshape(rows, BRANCH_W)
    return wk.astype(bf16), wv.astype(bf16)


def _rel_bias_table(rel_bias):
    t = jnp.arange(A_KBLOCKS)[:, None, None]
    r = jnp.arange(A_TQ)[None, :, None]
    c = jnp.arange(A_TQ)[None, None, :]
    rel = r - c + (A_KBLOCKS - 1 - t) * A_TQ
    idx = jnp.clip(rel, A_REL_MIN, A_REL_MAX) - A_REL_MIN
    dchunk = 2 * (A_KBLOCKS - 1 - t) + r // CHUNK - c // CHUNK
    valid = (dchunk >= 0) & (dchunk <= A_LEFT_CHUNKS)
    table = rel_bias.astype(f32)[:, idx] * LOG2E
    return jnp.where(valid[None], table, -jnp.inf)


def kernel(x, mem, positions, ln_mix_g, ln_mix_b, w_in, b_gate, b_forget, a_rel_bias, b_q_norm, b_kv_norm, b_w_uq, b_w_ukv, w_branch, w_mix_out, ln_xa_g, ln_xa_b, xa_w_q, xa_w_kv, xa_w_o, ln_ffn_g, ln_ffn_b, ffn_w_gu, ffn_w_down):
    xf = x.reshape(TOKENS, D_MODEL)
    memf = mem.reshape(BATCH * MEM_LEN, D_MODEL)
    pos = positions.reshape(TOKENS, 1)

    half = B_ROPE // 2
    inv_freq = ROPE_BASE ** (-jnp.arange(half, dtype=f32) / half)
    freq_row = jnp.concatenate([jnp.tile(inv_freq, 4), jnp.zeros((64,), f32)]).reshape(1, 128)
    sign_row = jnp.concatenate([jnp.tile(jnp.concatenate([-jnp.ones((half,), f32), jnp.ones((half,), f32)]), 2),
                                jnp.zeros((64,), f32)]).reshape(1, 128)
    nq = SEQ // FLASH_T

    for l in range(DEPTH):
        w_in_r = _arrange_w_in(w_in[l])
        qkv_a, qkv_c, gates, lat = _inproj(xf, w_in_r, b_gate[l].reshape(1, 3 * D_MODEL))

        wq1, wq2 = _arrange_w_uq(b_w_uq[l])
        wk, wv = _arrange_w_ukv(b_w_ukv[l])
        q_b, k_b, v_b = _latent(lat, pos, freq_row, sign_row,
                                b_q_norm[l].reshape(1, B_Q_LORA), b_kv_norm[l].reshape(1, B_KV_LORA),
                                wq1, wq2, wk, wv)

        cf_t = lat[:, 704:712].reshape(BATCH, SEQ, HEADS).transpose(0, 2, 1).reshape(BATCH * HEADS, SEQ)
        bf_col = jnp.tile(b_forget[l], BATCH).reshape(BATCH * HEADS, 1)
        forget = _forget_cumsum(cf_t, bf_col).reshape(BATCH, PAIRS, 2, nq, 1, FLASH_T)

        y_a = _attn_a(qkv_a, _rel_bias_table(a_rel_bias[l]))
        y_b = _flash(q_b, k_b, v_b, None, mode="mla", dq=B_PAIR_W, kcol0=0, vcol0=0)
        y_c = _flash(qkv_c, qkv_c, qkv_c, forget, mode="fox", dq=PAIR_W, kcol0=PAIRS, vcol0=2 * PAIRS)

        xf = _merge(xf, y_a, y_b, y_c, gates, w_branch[l].astype(bf16), w_mix_out[l].astype(bf16),
                    ln_mix_g[l].reshape(1, D_MODEL), ln_mix_b[l].reshape(1, D_MODEL))
        xf = _xattn(xf, memf, xa_w_q[l].astype(bf16), xa_w_kv[l].astype(bf16), xa_w_o[l].astype(bf16),
                    ln_xa_g[l].reshape(1, D_MODEL), ln_xa_b[l].reshape(1, D_MODEL))
        xf = _ffn(xf, ffn_w_gu[l].astype(bf16), ffn_w_down[l].astype(bf16),
                  ln_ffn_g[l].reshape(1, D_MODEL), ln_ffn_b[l].reshape(1, D_MODEL))
    return xf.reshape(BATCH, SEQ, D_MODEL)
```

```python
import functools
import math

import jax
import jax.numpy as jnp
from jax import lax
from jax.experimental import pallas as pl
from jax.experimental.pallas import tpu as pltpu

D_MODEL = 1024
BATCH = 8
SEQ = 2048
DEPTH = 2
TOKENS = BATCH * SEQ
CHUNK = 64
MEM_LEN = 256

HEADS = 8
HEAD_DIM = 64
PAIRS = HEADS // 2
PAIR_W = 2 * HEAD_DIM
BRANCH_W = HEADS * HEAD_DIM

A_LEFT_CHUNKS = 8
A_REL_MIN = -(CHUNK - 1)
A_REL_MAX = 256
ATT_T = 256
A_WINDOW = 3 * ATT_T

B_Q_LORA = 384
B_KV_LORA = 256
B_NOPE = 64
B_ROPE = 32
B_PAIR_W = 256
ROPE_BASE = 10000.0

XA_HEADS = 4
XA_HEAD_DIM = 128
FFN_HIDDEN = 2816
FFN_CHUNK = 256

LN_EPS = 1e-5
RMS_EPS = 1e-6
ALPHA = (2 * DEPTH) ** 0.25
LOG2E = math.log2(math.e)

COL_A = 0
COL_C = 1536
COL_G = 3072
COL_CQ = 6144
COL_CKV = 6528
COL_KR = 6784
IN_COLS_PADDED = 7040
LAT_W = 896

VMEM_LIMIT = 56 * 1024 * 1024

bf16 = jnp.bfloat16
f32 = jnp.float32


def _dot(a, b):
    return jnp.dot(a, b, preferred_element_type=f32)


def _dot_t(a, b):
    return lax.dot_general(a, b, (((1,), (1,)), ((), ())), preferred_element_type=f32)


def _layer_norm(z, g, b):
    mu = jnp.mean(z, axis=-1, keepdims=True)
    zc = z - mu
    var = jnp.mean(zc * zc, axis=-1, keepdims=True)
    return zc * lax.rsqrt(var + LN_EPS) * g + b


def _rms_norm(z, g):
    ms = jnp.mean(z * z, axis=-1, keepdims=True)
    return z * lax.rsqrt(ms + RMS_EPS) * g


def _params(*sem):
    return pltpu.CompilerParams(dimension_semantics=sem, vmem_limit_bytes=VMEM_LIMIT)


def _resident(shape):
    zeros = (0,) * len(shape)
    return pl.BlockSpec(shape, lambda *_: zeros, pipeline_mode=pl.Buffered(1))


def _inproj_kernel(x_ref, w_ref, bg_ref, qkva_ref, qkvc_ref, gate_ref, lat_ref):
    xb = x_ref[...].astype(bf16)
    qk_scale = HEAD_DIM ** -0.5 * LOG2E

    def mm(c0, n):
        return _dot(xb, w_ref[:, c0:c0 + n])

    for dst, base in ((qkva_ref, COL_A), (qkvc_ref, COL_C)):
        dst[:, 0:512] = (mm(base, 512) * qk_scale).astype(bf16)
        dst[:, 512:1024] = mm(base + 512, 512).astype(bf16)
        dst[:, 1024:1536] = mm(base + 1024, 512).astype(bf16)
    for c in range(6):
        z = mm(COL_G + c * 512, 512) + bg_ref[:, c * 512:(c + 1) * 512]
        gate_ref[:, c * 512:(c + 1) * 512] = (1.0 / (1.0 + jnp.exp(-z))).astype(bf16)
    lat_ref[:, 0:384] = mm(COL_CQ, 384)
    lat_ref[:, 384:640] = mm(COL_CKV, 256)
    lat_ref[:, 640:896] = mm(COL_KR, 256)


def _inproj(x, w, bg):
    tm = 512
    return pl.pallas_call(
        _inproj_kernel,
        grid=(TOKENS // tm,),
        in_specs=[
            pl.BlockSpec((tm, D_MODEL), lambda i: (i, 0)),
            _resident((D_MODEL, IN_COLS_PADDED)),
            _resident((1, 3 * D_MODEL)),
        ],
        out_specs=[
            pl.BlockSpec((tm, 1536), lambda i: (i, 0)),
            pl.BlockSpec((tm, 1536), lambda i: (i, 0)),
            pl.BlockSpec((tm, 3 * D_MODEL), lambda i: (i, 0)),
            pl.BlockSpec((tm, LAT_W), lambda i: (i, 0)),
        ],
        out_shape=[
            jax.ShapeDtypeStruct((TOKENS, 1536), bf16),
            jax.ShapeDtypeStruct((TOKENS, 1536), bf16),
            jax.ShapeDtypeStruct((TOKENS, 3 * D_MODEL), bf16),
            jax.ShapeDtypeStruct((TOKENS, LAT_W), f32),
        ],
        compiler_params=_params("parallel"),
        name="inproj",
    )(x, w, bg)


def _latent_kernel(lat_ref, pos_ref, freq_ref, sign_ref, gq_ref, gkv_ref,
                   wq1_ref, wq2_ref, wk_ref, wv_ref, qb_ref, kb_ref, vb_ref):
    ang = pos_ref[...].astype(f32) * freq_ref[...]
    live = (freq_ref[...] > 0.0).astype(f32)
    cos_t = jnp.cos(ang) * live
    sin_t = jnp.sin(ang) * sign_ref[...]
    scale = (B_NOPE + B_ROPE) ** -0.5 * LOG2E

    cqn = _rms_norm(lat_ref[:, 0:384], gq_ref[...]).astype(bf16)
    q1 = _dot(cqn, wq1_ref[...])
    q2 = _dot(cqn, wq2_ref[...])
    ckvn = _rms_norm(lat_ref[:, 384:640], gkv_ref[...]).astype(bf16)
    kn = _dot(ckvn, wk_ref[...])
    vb_ref[...] = _dot(ckvn, wv_ref[...]).astype(bf16)
    kpe = (lat_ref[:, 640:768] * cos_t + lat_ref[:, 768:896] * sin_t).astype(bf16)
    for p in range(PAIRS):
        c0 = p * B_PAIR_W
        qb_ref[:, c0:c0 + 128] = (q1[:, c0:c0 + 128] * scale).astype(bf16)
        pe = q1[:, c0 + 128:c0 + 256] * cos_t + q2[:, p * 128:(p + 1) * 128] * sin_t
        qb_ref[:, c0 + 128:c0 + 256] = (pe * scale).astype(bf16)
        kb_ref[:, c0:c0 + 128] = kn[:, p * 128:(p + 1) * 128].astype(bf16)
        kb_ref[:, c0 + 128:c0 + 256] = kpe


def _latent(lat, pos, freq, sign, gq, gkv, wq1, wq2, wk, wv):
    tm = 512
    row = lambda i: (i, 0)
    return pl.pallas_call(
        _latent_kernel,
        grid=(TOKENS // tm,),
        in_specs=[
            pl.BlockSpec((tm, LAT_W), row),
            pl.BlockSpec((tm, 1), row),
            _resident((1, 128)),
            _resident((1, 128)),
            _resident((1, B_Q_LORA)),
            _resident((1, B_KV_LORA)),
            _resident((B_Q_LORA, PAIRS * B_PAIR_W)),
            _resident((B_Q_LORA, PAIRS * 128)),
            _resident((B_KV_LORA, BRANCH_W)),
            _resident((B_KV_LORA, BRANCH_W)),
        ],
        out_specs=[
            pl.BlockSpec((tm, PAIRS * B_PAIR_W), row),
            pl.BlockSpec((tm, PAIRS * B_PAIR_W), row),
            pl.BlockSpec((tm, BRANCH_W), row),
        ],
        out_shape=[
            jax.ShapeDtypeStruct((TOKENS, PAIRS * B_PAIR_W), bf16),
            jax.ShapeDtypeStruct((TOKENS, PAIRS * B_PAIR_W), bf16),
            jax.ShapeDtypeStruct((TOKENS, BRANCH_W), bf16),
        ],
        compiler_params=_params("parallel"),
        name="latent",
    )(lat, pos, freq, sign, gq, gkv, wq1, wq2, wk, wv)


def _forget_kernel(cf_ref, bf_ref, o_ref):
    rows = cf_ref.shape[0]
    lane = lax.broadcasted_iota(jnp.int32, (rows, 128), 1)
    carry = jnp.zeros((rows, 1), f32)
    for blk in range(SEQ // 128):
        z = cf_ref[:, blk * 128:(blk + 1) * 128] + bf_ref[...]
        acc = jnp.minimum(z, 0.0) - jnp.log(1.0 + jnp.exp(-jnp.abs(z)))
        d = 1
        while d < 128:
            acc = acc + jnp.where(lane >= d, pltpu.roll(acc, d, 1), 0.0)
            d *= 2
        acc = acc + carry
        o_ref[:, blk * 128:(blk + 1) * 128] = acc
        carry = acc[:, 127:128]


def _forget_cumsum(cf_t, bf_col):
    rows = cf_t.shape[0]
    return pl.pallas_call(
        _forget_kernel,
        out_shape=jax.ShapeDtypeStruct((rows, SEQ), f32),
        name="forget_cumsum",
    )(cf_t, bf_col)


def _rel_table_kernel(w_ref, o_ref):
    x = jnp.broadcast_to(w_ref[...], (ATT_T, 1024))
    row = lax.broadcasted_iota(jnp.int32, (ATT_T, 1024), 0)
    x = pltpu.roll(x, 1024 - ATT_T, 1)
    for bit in range(8):
        x = jnp.where(((row >> bit) & 1) == 1, pltpu.roll(x, 1 << bit, 1), x)
    r = lax.broadcasted_iota(jnp.int32, (ATT_T, A_WINDOW), 0)
    c = lax.broadcasted_iota(jnp.int32, (ATT_T, A_WINDOW), 1)
    dchunk = (c >> 6) - (r >> 6)
    valid = (dchunk >= 0) & (dchunk <= A_LEFT_CHUNKS)
    o_ref[...] = jnp.where(valid, x[:, 0:A_WINDOW] * LOG2E, -jnp.inf)


def _rel_table(w_rows):
    return pl.pallas_call(
        _rel_table_kernel,
        grid=(HEADS,),
        in_specs=[pl.BlockSpec((None, 1, 1024), lambda h: (h, 0, 0))],
        out_specs=pl.BlockSpec((None, ATT_T, A_WINDOW), lambda h: (h, 0, 0)),
        out_shape=jax.ShapeDtypeStruct((HEADS, ATT_T, A_WINDOW), f32),
        name="rel_table",
    )(w_rows)


def _pair_select(x, width):
    lane = lax.broadcasted_iota(jnp.int32, (1, width), 1)
    if width == PAIR_W:
        in_a = lane < HEAD_DIM
        in_b = lane >= HEAD_DIM
    else:
        in_a = (lane < 64) | ((lane >= 128) & (lane < 160))
        in_b = ((lane >= 64) & (lane < 128)) | ((lane >= 160) & (lane < 192))
    zero = jnp.zeros_like(x)
    return jnp.where(in_a, x, zero), jnp.where(in_b, x, zero)


def _values_with_ones(v):
    lane = lax.broadcasted_iota(jnp.int32, (1, PAIR_W), 1)
    one = jnp.ones_like(v)
    return jnp.where(lane < HEAD_DIM, v, one), jnp.where(lane < HEAD_DIM, one, v)


def _normalise_pair(acc_a, acc_b):
    lane = lax.broadcasted_iota(jnp.int32, (1, PAIR_W), 1)
    ra = acc_a / pltpu.roll(acc_a, HEAD_DIM, 1)
    rb = acc_b / pltpu.roll(acc_b, HEAD_DIM, 1)
    return jnp.where(lane < HEAD_DIM, ra, rb)


def _attn_kernel(*refs, mode, dq):
    if mode == "mla":
        q_ref, k_ref, v_ref, o_ref = refs
        extra_ref = None
    else:
        q_ref, k_ref, v_ref, extra_ref, o_ref = refs
    i = pl.program_id(2)
    nq = SEQ // ATT_T
    neg = jnp.float32(-jnp.inf)

    def attend(k0, nkeys, q0):
        qa, qb = _pair_select(q_ref[...], dq)
        kt = k_ref[pl.ds(k0, nkeys), :]
        v_a, v_b = _values_with_ones(v_ref[pl.ds(k0, nkeys), :])
        outs = []
        for head, (qh, vh) in enumerate(((qa, v_a), (qb, v_b))):
            s = _dot_t(qh, kt)
            if mode == "rel":
                s = s + extra_ref[head, :, A_WINDOW - nkeys:A_WINDOW]
            elif mode == "fox":
                f = extra_ref[head]
                s = s + (f[:, q0:q0 + 1] - f[:, 0:nkeys]) * LOG2E
            if mode != "rel":
                row = lax.broadcasted_iota(jnp.int32, (ATT_T, ATT_T), 0)
                col = lax.broadcasted_iota(jnp.int32, (ATT_T, ATT_T), 1)
                keep = (col <= row) if mode == "fox" else ((col >> 6) <= (row >> 6))
                tail = jnp.where(keep, s[:, nkeys - ATT_T:], neg)
                s = tail if nkeys == ATT_T else jnp.concatenate([s[:, :nkeys - ATT_T], tail], axis=1)
            m = jnp.max(s, axis=-1, keepdims=True)
            outs.append(_dot(jnp.exp2(s - m).astype(bf16), vh))
        o_ref[...] = _normalise_pair(outs[0], outs[1]).astype(bf16)

    if mode == "rel":
        for ii in range(2):
            pl.when(i == ii)(functools.partial(attend, 0, (ii + 1) * ATT_T, 0))

        @pl.when(i >= 2)
        def _():
            attend(pl.multiple_of((i - 2) * ATT_T, ATT_T), A_WINDOW, 0)
    else:
        for ii in range(nq):
            pl.when(i == ii)(functools.partial(attend, 0, (ii + 1) * ATT_T, ii * ATT_T))


def _attention(q, k, v, extra, *, mode, dq, kcol0, vcol0):
    nq = SEQ // ATT_T
    in_specs = [
        pl.BlockSpec((ATT_T, dq), lambda b, j, i: (b * nq + i, j)),
        pl.BlockSpec((SEQ, dq), lambda b, j, i: (b, kcol0 + j)),
        pl.BlockSpec((SEQ, PAIR_W), lambda b, j, i: (b, vcol0 + j)),
    ]
    args = [q, k, v]
    if mode == "fox":
        in_specs.append(pl.BlockSpec((None, None, 2, 1, SEQ), lambda b, j, i: (b, j, 0, 0, 0)))
        args.append(extra)
    elif mode == "rel":
        in_specs.append(pl.BlockSpec((2, ATT_T, A_WINDOW), lambda b, j, i: (j, 0, 0)))
        args.append(extra)
    return pl.pallas_call(
        functools.partial(_attn_kernel, mode=mode, dq=dq),
        grid=(BATCH, PAIRS, nq),
        in_specs=in_specs,
        out_specs=pl.BlockSpec((ATT_T, PAIR_W), lambda b, j, i: (b * nq + i, j)),
        out_shape=jax.ShapeDtypeStruct((TOKENS, BRANCH_W), bf16),
        compiler_params=_params("parallel", "parallel", "arbitrary"),
        name="attn_" + mode,
    )(*args)


def _merge_kernel(x_ref, ya_ref, yb_ref, yc_ref, gate_ref, wb_ref, wo_ref, g_ref, b_ref, o_ref):
    merged = None
    for n, y_ref in enumerate((ya_ref, yb_ref, yc_ref)):
        proj = _dot(y_ref[...], wb_ref[n])
        term = gate_ref[:, n * D_MODEL:(n + 1) * D_MODEL].astype(f32) * proj
        merged = term if merged is None else merged + term
    y = _dot(merged.astype(bf16), wo_ref[...])
    o_ref[...] = _layer_norm(ALPHA * x_ref[...] + y, g_ref[...], b_ref[...])


def _merge(x, ya, yb, yc, gates, wb, wo, g, b):
    tm = 512
    row = lambda i: (i, 0)
    return pl.pallas_call(
        _merge_kernel,
        grid=(TOKENS // tm,),
        in_specs=[
            pl.BlockSpec((tm, D_MODEL), row),
            pl.BlockSpec((tm, BRANCH_W), row),
            pl.BlockSpec((tm, BRANCH_W), row),
            pl.BlockSpec((tm, BRANCH_W), row),
            pl.BlockSpec((tm, 3 * D_MODEL), row),
            _resident((3, BRANCH_W, D_MODEL)),
            _resident((D_MODEL, D_MODEL)),
            _resident((1, D_MODEL)),
            _resident((1, D_MODEL)),
        ],
        out_specs=pl.BlockSpec((tm, D_MODEL), row),
        out_shape=jax.ShapeDtypeStruct((TOKENS, D_MODEL), f32),
        compiler_params=_params("parallel"),
        name="merge",
    )(x, ya, yb, yc, gates, wb, wo, g, b)


def _xattn_kernel(x_ref, mem_ref, wq_ref, wkv_ref, wo_ref, g_ref, b_ref, o_ref, kv_ref):
    @pl.when(pl.program_id(1) == 0)
    def _():
        kv_ref[...] = _dot(mem_ref[...].astype(bf16), wkv_ref[...]).astype(bf16)

    x = x_ref[...]
    scale = XA_HEAD_DIM ** -0.5 * LOG2E
    q = (_dot(x.astype(bf16), wq_ref[...]) * scale).astype(bf16)
    width = XA_HEADS * XA_HEAD_DIM
    outs = []
    for h in range(XA_HEADS):
        c0 = h * XA_HEAD_DIM
        s = _dot_t(q[:, c0:c0 + XA_HEAD_DIM], kv_ref[:, c0:c0 + XA_HEAD_DIM])
        p = jnp.exp2(s - jnp.max(s, axis=-1, keepdims=True))
        l = jnp.sum(p, axis=-1, keepdims=True)
        o = _dot(p.astype(bf16), kv_ref[:, width + c0:width + c0 + XA_HEAD_DIM])
        outs.append((o / l).astype(bf16))
    y = _dot(jnp.concatenate(outs, axis=-1), wo_ref[...])
    o_ref[...] = _layer_norm(ALPHA * x + y, g_ref[...], b_ref[...])


def _xattn(x, mem, wq, wkv, wo, g, b):
    tm = 512
    nq = SEQ // tm
    width = XA_HEADS * XA_HEAD_DIM
    return pl.pallas_call(
        _xattn_kernel,
        grid=(BATCH, nq),
        in_specs=[
            pl.BlockSpec((tm, D_MODEL), lambda bi, i: (bi * nq + i, 0)),
            pl.BlockSpec((MEM_LEN, D_MODEL), lambda bi, i: (bi, 0)),
            _resident((D_MODEL, width)),
            _resident((D_MODEL, 2 * width)),
            _resident((width, D_MODEL)),
            _resident((1, D_MODEL)),
            _resident((1, D_MODEL)),
        ],
        out_specs=pl.BlockSpec((tm, D_MODEL), lambda bi, i: (bi * nq + i, 0)),
        out_shape=jax.ShapeDtypeStruct((TOKENS, D_MODEL), f32),
        scratch_shapes=[pltpu.VMEM((MEM_LEN, 2 * width), bf16)],
        compiler_params=_params("arbitrary", "arbitrary"),
        name="xattn",
    )(x, mem, wq, wkv, wo, g, b)


def _ffn_kernel(x_ref, wgu_ref, wd_ref, g_ref, b_ref, o_ref, acc_ref):
    x = x_ref[...]
    xb = x.astype(bf16)
    for c in range(FFN_HIDDEN // FFN_CHUNK):
        c0 = c * FFN_CHUNK
        gate = _dot(xb, wgu_ref[:, c0:c0 + FFN_CHUNK])
        up = _dot(xb, wgu_ref[:, FFN_HIDDEN + c0:FFN_HIDDEN + c0 + FFN_CHUNK])
        h = (gate / (1.0 + jnp.exp(-gate)) * up).astype(bf16)
        part = _dot(h, wd_ref[c0:c0 + FFN_CHUNK, :])
        if c == 0:
            acc_ref[...] = part
        else:
            acc_ref[...] += part
    o_ref[...] = _layer_norm(ALPHA * x + acc_ref[...], g_ref[...], b_ref[...])


def _ffn(x, wgu, wd, g, b):
    tm = 512
    row = lambda i: (i, 0)
    return pl.pallas_call(
        _ffn_kernel,
        grid=(TOKENS // tm,),
        in_specs=[
            pl.BlockSpec((tm, D_MODEL), row),
            _resident((D_MODEL, 2 * FFN_HIDDEN)),
            _resident((FFN_HIDDEN, D_MODEL)),
            _resident((1, D_MODEL)),
            _resident((1, D_MODEL)),
        ],
        out_specs=pl.BlockSpec((tm, D_MODEL), row),
        out_shape=jax.ShapeDtypeStruct((TOKENS, D_MODEL), f32),
        scratch_shapes=[pltpu.VMEM((tm, D_MODEL), f32)],
        compiler_params=_params("parallel"),
        name="ffn",
    )(x, wgu, wd, g, b)


def _arrange_w_in(w):
    a = w[:, 0:1536]
    cq = w[:, 1536:1920]
    ckv = w[:, 1920:2176]
    kr = w[:, 2176:2208]
    c = w[:, 2208:3744]
    cf = w[:, 3744:3752]
    g = w[:, 3752:6824]
    half = B_ROPE // 2
    kr_sw = jnp.concatenate([kr[:, half:], kr[:, :half]], axis=1)
    rows = w.shape[0]
    kra = jnp.concatenate([kr, kr, cf, jnp.zeros((rows, 56), w.dtype)], axis=1)
    krb = jnp.concatenate([kr_sw, kr_sw, jnp.zeros((rows, 64), w.dtype)], axis=1)
    return jnp.concatenate([a, c, g, cq, ckv, kra, krb], axis=1).astype(bf16)


def _arrange_w_uq(w):
    rows = w.shape[0]
    w = w.reshape(rows, HEADS, B_NOPE + B_ROPE)
    nope = w[:, :, :B_NOPE]
    pe = w[:, :, B_NOPE:]
    half = B_ROPE // 2
    pe_sw = jnp.concatenate([pe[:, :, half:], pe[:, :, :half]], axis=2)
    z64 = jnp.zeros((rows, 64), w.dtype)
    main, swapped = [], []
    for p in range(PAIRS):
        a, b = 2 * p, 2 * p + 1
        main += [nope[:, a], nope[:, b], pe[:, a], pe[:, b], z64]
        swapped += [pe_sw[:, a], pe_sw[:, b], z64]
    return (jnp.concatenate(main, axis=1).astype(bf16),
            jnp.concatenate(swapped, axis=1).astype(bf16))


def _arrange_w_ukv(w):
    rows = w.shape[0]
    w = w.reshape(rows, HEADS, B_NOPE + HEAD_DIM)
    wk = w[:, :, :B_NOPE].reshape(rows, BRANCH_W)
    wv = w[:, :, B_NOPE:].reshape(rows, BRANCH_W)
    return wk.astype(bf16), wv.astype(bf16)


def _rel_bias_rows(rel_bias):
    lo = jnp.broadcast_to(rel_bias[:, :1], (HEADS, 1024 - 512 - rel_bias.shape[1]))
    hi = jnp.broadcast_to(rel_bias[:, -1:], (HEADS, 512))
    return jnp.concatenate([hi, rel_bias[:, ::-1], lo], axis=1).astype(f32).reshape(HEADS, 1, 1024)


def kernel(x, mem, positions, ln_mix_g, ln_mix_b, w_in, b_gate, b_forget, a_rel_bias, b_q_norm, b_kv_norm, b_w_uq, b_w_ukv, w_branch, w_mix_out, ln_xa_g, ln_xa_b, xa_w_q, xa_w_kv, xa_w_o, ln_ffn_g, ln_ffn_b, ffn_w_gu, ffn_w_down):
    xf = x.reshape(TOKENS, D_MODEL)
    memf = mem.reshape(BATCH * MEM_LEN, D_MODEL)
    pos = positions.reshape(TOKENS, 1)

    half = B_ROPE // 2
    inv_freq = ROPE_BASE ** (-jnp.arange(half, dtype=f32) / half)
    freq_row = jnp.concatenate([jnp.tile(inv_freq, 4), jnp.zeros((64,), f32)]).reshape(1, 128)
    sign_row = jnp.concatenate([jnp.tile(jnp.concatenate([-jnp.ones((half,), f32), jnp.ones((half,), f32)]), 2),
                                jnp.zeros((64,), f32)]).reshape(1, 128)
    for l in range(DEPTH):
        w_in_r = _arrange_w_in(w_in[l])
        qkv_a, qkv_c, gates, lat = _inproj(xf, w_in_r, b_gate[l].reshape(1, 3 * D_MODEL))

        wq1, wq2 = _arrange_w_uq(b_w_uq[l])
        wk, wv = _arrange_w_ukv(b_w_ukv[l])
        q_b, k_b, v_b = _latent(lat, pos, freq_row, sign_row,
                                b_q_norm[l].reshape(1, B_Q_LORA), b_kv_norm[l].reshape(1, B_KV_LORA),
                                wq1, wq2, wk, wv)

        cf_t = lat[:, 704:712].reshape(BATCH, SEQ, HEADS).transpose(0, 2, 1).reshape(BATCH * HEADS, SEQ)
        bf_col = jnp.tile(b_forget[l], BATCH).reshape(BATCH * HEADS, 1)
        forget = _forget_cumsum(cf_t, bf_col).reshape(BATCH, PAIRS, 2, 1, SEQ)

        rel_table = _rel_table(_rel_bias_rows(a_rel_bias[l]))
        y_a = _attention(qkv_a, qkv_a, qkv_a, rel_table, mode="rel", dq=PAIR_W, kcol0=PAIRS, vcol0=2 * PAIRS)
        y_b = _attention(q_b, k_b, v_b, None, mode="mla", dq=B_PAIR_W, kcol0=0, vcol0=0)
        y_c = _attention(qkv_c, qkv_c, qkv_c, forget, mode="fox", dq=PAIR_W, kcol0=PAIRS, vcol0=2 * PAIRS)

        xf = _merge(xf, y_a, y_b, y_c, gates, w_branch[l].astype(bf16), w_mix_out[l].astype(bf16),
                    ln_mix_g[l].reshape(1, D_MODEL), ln_mix_b[l].reshape(1, D_MODEL))
        xf = _xattn(xf, memf, xa_w_q[l].astype(bf16), xa_w_kv[l].astype(bf16), xa_w_o[l].astype(bf16),
                    ln_xa_g[l].reshape(1, D_MODEL), ln_xa_b[l].reshape(1, D_MODEL))
        xf = _ffn(xf, ffn_w_gu[l].astype(bf16), ffn_w_down[l].astype(bf16),
                  ln_ffn_g[l].reshape(1, D_MODEL), ln_ffn_b[l].reshape(1, D_MODEL))
    return xf.reshape(BATCH, SEQ, D_MODEL)
```

```python
import functools
import math

import jax
import jax.numpy as jnp
from jax import lax
from jax.experimental import pallas as pl
from jax.experimental.pallas import tpu as pltpu

D_MODEL = 1024
BATCH = 8
SEQ = 2048
DEPTH = 2
TOKENS = BATCH * SEQ
CHUNK = 64
MEM_LEN = 256

HEADS = 8
HEAD_DIM = 64
PAIRS = HEADS // 2
PAIR_W = 2 * HEAD_DIM
BRANCH_W = HEADS * HEAD_DIM

A_LEFT_CHUNKS = 8
A_REL_MIN = -(CHUNK - 1)
A_REL_MAX = 256
ATT_T = 256
A_WINDOW = 3 * ATT_T
CAUSAL_T = 512

B_Q_LORA = 384
B_KV_LORA = 256
B_NOPE = 64
B_ROPE = 32
B_PAIR_W = 256
ROPE_BASE = 10000.0

XA_HEADS = 4
XA_HEAD_DIM = 128
FFN_HIDDEN = 2816
FFN_CHUNK = 256

LN_EPS = 1e-5
RMS_EPS = 1e-6
ALPHA = (2 * DEPTH) ** 0.25
LOG2E = math.log2(math.e)

LAT_W = 896

VMEM_LIMIT = 56 * 1024 * 1024

bf16 = jnp.bfloat16
f32 = jnp.float32


def _dot(a, b):
    return jnp.dot(a, b, preferred_element_type=f32)


def _dot_t(a, b):
    return lax.dot_general(a, b, (((1,), (1,)), ((), ())), preferred_element_type=f32)


def _layer_norm(z, g, b):
    mu = jnp.mean(z, axis=-1, keepdims=True)
    zc = z - mu
    var = jnp.mean(zc * zc, axis=-1, keepdims=True)
    return zc * lax.rsqrt(var + LN_EPS) * g + b


def _rms_norm(z, g):
    ms = jnp.mean(z * z, axis=-1, keepdims=True)
    return z * lax.rsqrt(ms + RMS_EPS) * g


def _params(*sem):
    return pltpu.CompilerParams(dimension_semantics=sem, vmem_limit_bytes=VMEM_LIMIT)


def _resident(shape):
    zeros = (0,) * len(shape)
    return pl.BlockSpec(shape, lambda *_: zeros, pipeline_mode=pl.Buffered(1))


def _inproj_kernel(x_ref, wa_ref, wc_ref, wg_ref, wl_ref, bg_ref, qkva_ref, qkvc_ref, gate_ref, lat_ref):
    xb = x_ref[...].astype(bf16)
    qk_scale = HEAD_DIM ** -0.5 * LOG2E

    for dst, w_ref in ((qkva_ref, wa_ref), (qkvc_ref, wc_ref)):
        dst[:, 0:512] = (_dot(xb, w_ref[:, 0:512]) * qk_scale).astype(bf16)
        dst[:, 512:1024] = _dot(xb, w_ref[:, 512:1024]).astype(bf16)
        dst[:, 1024:1536] = _dot(xb, w_ref[:, 1024:1536]).astype(bf16)
    for c in range(6):
        z = _dot(xb, wg_ref[:, c * 512:(c + 1) * 512]) + bg_ref[:, c * 512:(c + 1) * 512]
        gate_ref[:, c * 512:(c + 1) * 512] = (1.0 / (1.0 + jnp.exp(-z))).astype(bf16)
    lat_ref[:, 0:384] = _dot(xb, wl_ref[:, 0:384])
    lat_ref[:, 384:640] = _dot(xb, wl_ref[:, 384:640])
    lat_ref[:, 640:896] = _dot(xb, wl_ref[:, 640:896])


def _inproj(x, wa, wc, wg, wl, bg):
    tm = 512
    return pl.pallas_call(
        _inproj_kernel,
        grid=(TOKENS // tm,),
        in_specs=[
            pl.BlockSpec((tm, D_MODEL), lambda i: (i, 0)),
            _resident((D_MODEL, 1536)),
            _resident((D_MODEL, 1536)),
            _resident((D_MODEL, 3 * D_MODEL)),
            _resident((D_MODEL, LAT_W)),
            _resident((1, 3 * D_MODEL)),
        ],
        out_specs=[
            pl.BlockSpec((tm, 1536), lambda i: (i, 0)),
            pl.BlockSpec((tm, 1536), lambda i: (i, 0)),
            pl.BlockSpec((tm, 3 * D_MODEL), lambda i: (i, 0)),
            pl.BlockSpec((tm, LAT_W), lambda i: (i, 0)),
        ],
        out_shape=[
            jax.ShapeDtypeStruct((TOKENS, 1536), bf16),
            jax.ShapeDtypeStruct((TOKENS, 1536), bf16),
            jax.ShapeDtypeStruct((TOKENS, 3 * D_MODEL), bf16),
            jax.ShapeDtypeStruct((TOKENS, LAT_W), f32),
        ],
        compiler_params=_params("parallel"),
        name="inproj",
    )(x, wa, wc, wg, wl, bg)


def _rope_kernel(pos_ref, freq_ref, sign_ref, cos_ref, sin_ref):
    ang = pos_ref[...].astype(f32) * freq_ref[...]
    live = (freq_ref[...] > 0.0).astype(f32)
    cos_ref[...] = jnp.cos(ang) * live
    sin_ref[...] = jnp.sin(ang) * sign_ref[...]


def _rope_tables(pos, freq, sign):
    tm = 2048
    row = lambda i: (i, 0)
    return pl.pallas_call(
        _rope_kernel,
        grid=(TOKENS // tm,),
        in_specs=[pl.BlockSpec((tm, 1), row), _resident((1, 128)), _resident((1, 128))],
        out_specs=[pl.BlockSpec((tm, 128), row), pl.BlockSpec((tm, 128), row)],
        out_shape=[jax.ShapeDtypeStruct((TOKENS, 128), f32), jax.ShapeDtypeStruct((TOKENS, 128), f32)],
        compiler_params=_params("parallel"),
        name="rope_tables",
    )(pos, freq, sign)


def _latent_kernel(lat_ref, cos_ref, sin_ref, gq_ref, gkv_ref,
                   wq1_ref, wq2_ref, wk_ref, wv_ref, qb_ref, kb_ref, vb_ref):
    cos_t = cos_ref[...]
    sin_t = sin_ref[...]
    scale = (B_NOPE + B_ROPE) ** -0.5 * LOG2E

    cqn = _rms_norm(lat_ref[:, 0:384], gq_ref[...]).astype(bf16)
    q1 = _dot(cqn, wq1_ref[...])
    q2 = _dot(cqn, wq2_ref[...])
    ckvn = _rms_norm(lat_ref[:, 384:640], gkv_ref[...]).astype(bf16)
    kn = _dot(ckvn, wk_ref[...])
    vb_ref[...] = _dot(ckvn, wv_ref[...]).astype(bf16)
    kpe = (lat_ref[:, 640:768] * cos_t + lat_ref[:, 768:896] * sin_t).astype(bf16)
    for p in range(PAIRS):
        c0 = p * B_PAIR_W
        qb_ref[:, c0:c0 + 128] = (q1[:, c0:c0 + 128] * scale).astype(bf16)
        pe = q1[:, c0 + 128:c0 + 256] * cos_t + q2[:, p * 128:(p + 1) * 128] * sin_t
        qb_ref[:, c0 + 128:c0 + 256] = (pe * scale).astype(bf16)
        kb_ref[:, c0:c0 + 128] = kn[:, p * 128:(p + 1) * 128].astype(bf16)
        kb_ref[:, c0 + 128:c0 + 256] = kpe


def _latent(lat, cos_t, sin_t, gq, gkv, wq1, wq2, wk, wv):
    tm = 512
    row = lambda i: (i, 0)
    return pl.pallas_call(
        _latent_kernel,
        grid=(TOKENS // tm,),
        in_specs=[
            pl.BlockSpec((tm, LAT_W), row),
            pl.BlockSpec((tm, 128), row),
            pl.BlockSpec((tm, 128), row),
            _resident((1, B_Q_LORA)),
            _resident((1, B_KV_LORA)),
            _resident((B_Q_LORA, PAIRS * B_PAIR_W)),
            _resident((B_Q_LORA, PAIRS * 128)),
            _resident((B_KV_LORA, BRANCH_W)),
            _resident((B_KV_LORA, BRANCH_W)),
        ],
        out_specs=[
            pl.BlockSpec((tm, PAIRS * B_PAIR_W), row),
            pl.BlockSpec((tm, PAIRS * B_PAIR_W), row),
            pl.BlockSpec((tm, BRANCH_W), row),
        ],
        out_shape=[
            jax.ShapeDtypeStruct((TOKENS, PAIRS * B_PAIR_W), bf16),
            jax.ShapeDtypeStruct((TOKENS, PAIRS * B_PAIR_W), bf16),
            jax.ShapeDtypeStruct((TOKENS, BRANCH_W), bf16),
        ],
        compiler_params=_params("parallel"),
        name="latent",
    )(lat, cos_t, sin_t, gq, gkv, wq1, wq2, wk, wv)


def _forget_kernel(cf_ref, bf_ref, o_ref):
    rows = cf_ref.shape[0]
    lane = lax.broadcasted_iota(jnp.int32, (rows, 128), 1)
    carry = jnp.zeros((rows, 1), f32)
    for blk in range(SEQ // 128):
        z = cf_ref[:, blk * 128:(blk + 1) * 128] + bf_ref[...]
        acc = jnp.minimum(z, 0.0) - jnp.log(1.0 + jnp.exp(-jnp.abs(z)))
        d = 1
        while d < 128:
            acc = acc + jnp.where(lane >= d, pltpu.roll(acc, d, 1), 0.0)
            d *= 2
        acc = acc + carry
        o_ref[:, blk * 128:(blk + 1) * 128] = acc
        carry = acc[:, 127:128]


def _forget_cumsum(cf_t, bf_col):
    rows = cf_t.shape[0]
    return pl.pallas_call(
        _forget_kernel,
        out_shape=jax.ShapeDtypeStruct((rows, SEQ), f32),
        name="forget_cumsum",
    )(cf_t, bf_col)


def _rel_table_kernel(w_ref, o_ref):
    x = jnp.broadcast_to(w_ref[...], (ATT_T, 1024))
    row = lax.broadcasted_iota(jnp.int32, (ATT_T, 1024), 0)
    x = pltpu.roll(x, 1024 - ATT_T, 1)
    for bit in range(8):
        x = jnp.where(((row >> bit) & 1) == 1, pltpu.roll(x, 1 << bit, 1), x)
    r = lax.broadcasted_iota(jnp.int32, (ATT_T, A_WINDOW), 0)
    c = lax.broadcasted_iota(jnp.int32, (ATT_T, A_WINDOW), 1)
    dchunk = (c >> 6) - (r >> 6)
    valid = (dchunk >= 0) & (dchunk <= A_LEFT_CHUNKS)
    o_ref[...] = jnp.where(valid, x[:, 0:A_WINDOW] * LOG2E, -jnp.inf)


def _rel_table(w_rows):
    return pl.pallas_call(
        _rel_table_kernel,
        grid=(HEADS,),
        in_specs=[pl.BlockSpec((None, 1, 1024), lambda h: (h, 0, 0))],
        out_specs=pl.BlockSpec((None, ATT_T, A_WINDOW), lambda h: (h, 0, 0)),
        out_shape=jax.ShapeDtypeStruct((HEADS, ATT_T, A_WINDOW), f32),
        name="rel_table",
    )(w_rows)


def _pair_select(x, width):
    lane = lax.broadcasted_iota(jnp.int32, (1, width), 1)
    if width == PAIR_W:
        in_a = lane < HEAD_DIM
        in_b = lane >= HEAD_DIM
    else:
        in_a = (lane < 64) | ((lane >= 128) & (lane < 160))
        in_b = ((lane >= 64) & (lane < 128)) | ((lane >= 160) & (lane < 192))
    zero = jnp.zeros_like(x)
    return jnp.where(in_a, x, zero), jnp.where(in_b, x, zero)


def _values_with_ones(v):
    lane = lax.broadcasted_iota(jnp.int32, (1, PAIR_W), 1)
    one = jnp.ones_like(v)
    return jnp.where(lane < HEAD_DIM, v, one), jnp.where(lane < HEAD_DIM, one, v)


def _normalise_pair(acc_a, acc_b):
    lane = lax.broadcasted_iota(jnp.int32, (1, PAIR_W), 1)
    ra = acc_a / pltpu.roll(acc_a, HEAD_DIM, 1)
    rb = acc_b / pltpu.roll(acc_b, HEAD_DIM, 1)
    return jnp.where(lane < HEAD_DIM, ra, rb)


def _attn_kernel(*refs, mode, dq, tq, pps):
    if mode == "mla":
        q_ref, k_ref, v_ref, o_ref = refs
        extra_ref = None
    else:
        q_ref, k_ref, v_ref, extra_ref, o_ref = refs
    i = pl.program_id(2)
    nq = SEQ // tq
    neg = jnp.float32(-jnp.inf)

    def attend(k0, nkeys, q0):
        for p in range(pps):
            qa, qb = _pair_select(q_ref[:, p * dq:(p + 1) * dq], dq)
            kt = k_ref[pl.ds(k0, nkeys), p * dq:(p + 1) * dq]
            v_a, v_b = _values_with_ones(v_ref[pl.ds(k0, nkeys), p * PAIR_W:(p + 1) * PAIR_W])
            outs = []
            for head, (qh, vh) in enumerate(((qa, v_a), (qb, v_b))):
                s = _dot_t(qh, kt)
                if mode == "rel":
                    s = s + extra_ref[2 * p + head, :, A_WINDOW - nkeys:A_WINDOW]
                elif mode == "fox":
                    f = extra_ref[p, head]
                    s = s + (f[:, q0:q0 + 1] - f[:, 0:nkeys]) * LOG2E
                if mode != "rel":
                    row = lax.broadcasted_iota(jnp.int32, (tq, tq), 0)
                    col = lax.broadcasted_iota(jnp.int32, (tq, tq), 1)
                    keep = (col <= row) if mode == "fox" else ((col >> 6) <= (row >> 6))
                    tail = jnp.where(keep, s[:, nkeys - tq:], neg)
                    s = tail if nkeys == tq else jnp.concatenate([s[:, :nkeys - tq], tail], axis=1)
                m = jnp.max(s, axis=-1, keepdims=True)
                outs.append(_dot(jnp.exp2(s - m).astype(bf16), vh))
            o_ref[:, p * PAIR_W:(p + 1) * PAIR_W] = _normalise_pair(outs[0], outs[1]).astype(bf16)

    if mode == "rel":
        for ii in range(2):
            pl.when(i == ii)(functools.partial(attend, 0, (ii + 1) * tq, 0))

        @pl.when(i >= 2)
        def _():
            attend(pl.multiple_of((i - 2) * tq, tq), A_WINDOW, 0)
    else:
        for ii in range(nq):
            pl.when(i == ii)(functools.partial(attend, 0, (ii + 1) * tq, ii * tq))


def _attention(q, k, v, extra, *, mode, dq, tq, pps, kcol0, vcol0):
    nq = SEQ // tq
    groups = PAIRS // pps
    in_specs = [
        pl.BlockSpec((tq, pps * dq), lambda b, j, i: (b * nq + i, j)),
        pl.BlockSpec((SEQ, pps * dq), lambda b, j, i: (b, kcol0 // pps + j)),
        pl.BlockSpec((SEQ, pps * PAIR_W), lambda b, j, i: (b, vcol0 // pps + j)),
    ]
    args = [q, k, v]
    if mode == "fox":
        in_specs.append(pl.BlockSpec((None, pps, 2, 1, SEQ), lambda b, j, i: (b, j, 0, 0, 0)))
        args.append(extra)
    elif mode == "rel":
        assert tq == ATT_T
        in_specs.append(pl.BlockSpec((2 * pps, ATT_T, A_WINDOW), lambda b, j, i: (j, 0, 0)))
        args.append(extra)
    return pl.pallas_call(
        functools.partial(_attn_kernel, mode=mode, dq=dq, tq=tq, pps=pps),
        grid=(BATCH, groups, nq),
        in_specs=in_specs,
        out_specs=pl.BlockSpec((tq, pps * PAIR_W), lambda b, j, i: (b * nq + i, j)),
        out_shape=jax.ShapeDtypeStruct((TOKENS, BRANCH_W), bf16),
        compiler_params=_params("parallel", "parallel", "arbitrary"),
        name="attn_" + mode,
    )(*args)


def _merge_kernel(x_ref, ya_ref, yb_ref, yc_ref, gate_ref, wb_ref, wo_ref, g_ref, b_ref, o_ref):
    merged = None
    for n, y_ref in enumerate((ya_ref, yb_ref, yc_ref)):
        proj = _dot(y_ref[...], wb_ref[n])
        term = gate_ref[:, n * D_MODEL:(n + 1) * D_MODEL].astype(f32) * proj
        merged = term if merged is None else merged + term
    y = _dot(merged.astype(bf16), wo_ref[...])
    o_ref[...] = _layer_norm(ALPHA * x_ref[...] + y, g_ref[...], b_ref[...])


def _merge(x, ya, yb, yc, gates, wb, wo, g, b):
    tm = 512
    row = lambda i: (i, 0)
    return pl.pallas_call(
        _merge_kernel,
        grid=(TOKENS // tm,),
        in_specs=[
            pl.BlockSpec((tm, D_MODEL), row),
            pl.BlockSpec((tm, BRANCH_W), row),
            pl.BlockSpec((tm, BRANCH_W), row),
            pl.BlockSpec((tm, BRANCH_W), row),
            pl.BlockSpec((tm, 3 * D_MODEL), row),
            _resident((3, BRANCH_W, D_MODEL)),
            _resident((D_MODEL, D_MODEL)),
            _resident((1, D_MODEL)),
            _resident((1, D_MODEL)),
        ],
        out_specs=pl.BlockSpec((tm, D_MODEL), row),
        out_shape=jax.ShapeDtypeStruct((TOKENS, D_MODEL), f32),
        compiler_params=_params("parallel"),
        name="merge",
    )(x, ya, yb, yc, gates, wb, wo, g, b)


def _xattn_kernel(x_ref, mem_ref, wq_ref, wkv_ref, wo_ref, g_ref, b_ref, o_ref, kv_ref):
    @pl.when(pl.program_id(1) == 0)
    def _():
        kv_ref[...] = _dot(mem_ref[...].astype(bf16), wkv_ref[...]).astype(bf16)

    x = x_ref[...]
    scale = XA_HEAD_DIM ** -0.5 * LOG2E
    q = (_dot(x.astype(bf16), wq_ref[...]) * scale).astype(bf16)
    width = XA_HEADS * XA_HEAD_DIM
    outs = []
    for h in range(XA_HEADS):
        c0 = h * XA_HEAD_DIM
        s = _dot_t(q[:, c0:c0 + XA_HEAD_DIM], kv_ref[:, c0:c0 + XA_HEAD_DIM])
        p = jnp.exp2(s - jnp.max(s, axis=-1, keepdims=True))
        l = jnp.sum(p, axis=-1, keepdims=True)
        o = _dot(p.astype(bf16), kv_ref[:, width + c0:width + c0 + XA_HEAD_DIM])
        outs.append((o / l).astype(bf16))
    y = _dot(jnp.concatenate(outs, axis=-1), wo_ref[...])
    o_ref[...] = _layer_norm(ALPHA * x + y, g_ref[...], b_ref[...])


def _xattn(x, mem, wq, wkv, wo, g, b):
    tm = 512
    nq = SEQ // tm
    width = XA_HEADS * XA_HEAD_DIM
    return pl.pallas_call(
        _xattn_kernel,
        grid=(BATCH, nq),
        in_specs=[
            pl.BlockSpec((tm, D_MODEL), lambda bi, i: (bi * nq + i, 0)),
            pl.BlockSpec((MEM_LEN, D_MODEL), lambda bi, i: (bi, 0)),
            _resident((D_MODEL, width)),
            _resident((D_MODEL, 2 * width)),
            _resident((width, D_MODEL)),
            _resident((1, D_MODEL)),
            _resident((1, D_MODEL)),
        ],
        out_specs=pl.BlockSpec((tm, D_MODEL), lambda bi, i: (bi * nq + i, 0)),
        out_shape=jax.ShapeDtypeStruct((TOKENS, D_MODEL), f32),
        scratch_shapes=[pltpu.VMEM((MEM_LEN, 2 * width), bf16)],
        compiler_params=_params("arbitrary", "arbitrary"),
        name="xattn",
    )(x, mem, wq, wkv, wo, g, b)


def _ffn_kernel(x_ref, wgu_ref, wd_ref, g_ref, b_ref, o_ref, acc_ref):
    x = x_ref[...]
    xb = x.astype(bf16)
    for c in range(FFN_HIDDEN // FFN_CHUNK):
        c0 = c * FFN_CHUNK
        gate = _dot(xb, wgu_ref[:, c0:c0 + FFN_CHUNK])
        up = _dot(xb, wgu_ref[:, FFN_HIDDEN + c0:FFN_HIDDEN + c0 + FFN_CHUNK])
        h = (gate / (1.0 + jnp.exp(-gate)) * up).astype(bf16)
        part = _dot(h, wd_ref[c0:c0 + FFN_CHUNK, :])
        if c == 0:
            acc_ref[...] = part
        else:
            acc_ref[...] += part
    o_ref[...] = _layer_norm(ALPHA * x + acc_ref[...], g_ref[...], b_ref[...])


def _ffn(x, wgu, wd, g, b):
    tm = 512
    row = lambda i: (i, 0)
    return pl.pallas_call(
        _ffn_kernel,
        grid=(TOKENS // tm,),
        in_specs=[
            pl.BlockSpec((tm, D_MODEL), row),
            _resident((D_MODEL, 2 * FFN_HIDDEN)),
            _resident((FFN_HIDDEN, D_MODEL)),
            _resident((1, D_MODEL)),
            _resident((1, D_MODEL)),
        ],
        out_specs=pl.BlockSpec((tm, D_MODEL), row),
        out_shape=jax.ShapeDtypeStruct((TOKENS, D_MODEL), f32),
        scratch_shapes=[pltpu.VMEM((tm, D_MODEL), f32)],
        compiler_params=_params("parallel"),
        name="ffn",
    )(x, wgu, wd, g, b)


def _arrange_w_in(w):
    a = w[:, 0:1536].astype(bf16)
    c = w[:, 2208:3744].astype(bf16)
    g = w[:, 3752:6824].astype(bf16)
    small = w[:, 1536:2208].astype(bf16)
    cq_ckv = small[:, 0:640]
    kr = small[:, 640:672]
    cf = w[:, 3744:3752].astype(bf16)
    half = B_ROPE // 2
    kr_sw = jnp.concatenate([kr[:, half:], kr[:, :half]], axis=1)
    rows = w.shape[0]
    lat = jnp.concatenate([cq_ckv, kr, kr, cf, jnp.zeros((rows, 56), bf16),
                           kr_sw, kr_sw, jnp.zeros((rows, 64), bf16)], axis=1)
    return a, c, g, lat


def _arrange_w_uq(w):
    rows = w.shape[0]
    w = w.reshape(rows, HEADS, B_NOPE + B_ROPE)
    nope = w[:, :, :B_NOPE]
    pe = w[:, :, B_NOPE:]
    half = B_ROPE // 2
    pe_sw = jnp.concatenate([pe[:, :, half:], pe[:, :, :half]], axis=2)
    z64 = jnp.zeros((rows, 64), w.dtype)
    main, swapped = [], []
    for p in range(PAIRS):
        a, b = 2 * p, 2 * p + 1
        main += [nope[:, a], nope[:, b], pe[:, a], pe[:, b], z64]
        swapped += [pe_sw[:, a], pe_sw[:, b], z64]
    return (jnp.concatenate(main, axis=1).astype(bf16),
            jnp.concatenate(swapped, axis=1).astype(bf16))


def _arrange_w_ukv(w):
    rows = w.shape[0]
    w = w.reshape(rows, HEADS, B_NOPE + HEAD_DIM)
    wk = w[:, :, :B_NOPE].reshape(rows, BRANCH_W)
    wv = w[:, :, B_NOPE:].reshape(rows, BRANCH_W)
    return wk.astype(bf16), wv.astype(bf16)


def _rel_bias_rows(rel_bias):
    lo = jnp.broadcast_to(rel_bias[:, :1], (HEADS, 1024 - 512 - rel_bias.shape[1]))
    hi = jnp.broadcast_to(rel_bias[:, -1:], (HEADS, 512))
    return jnp.concatenate([hi, rel_bias[:, ::-1], lo], axis=1).astype(f32).reshape(HEADS, 1, 1024)


def kernel(x, mem, positions, ln_mix_g, ln_mix_b, w_in, b_gate, b_forget, a_rel_bias, b_q_norm, b_kv_norm, b_w_uq, b_w_ukv, w_branch, w_mix_out, ln_xa_g, ln_xa_b, xa_w_q, xa_w_kv, xa_w_o, ln_ffn_g, ln_ffn_b, ffn_w_gu, ffn_w_down):
    xf = x.reshape(TOKENS, D_MODEL)
    memf = mem.reshape(BATCH * MEM_LEN, D_MODEL)
    pos = positions.reshape(TOKENS, 1)

    half = B_ROPE // 2
    inv_freq = ROPE_BASE ** (-jnp.arange(half, dtype=f32) / half)
    freq_row = jnp.concatenate([jnp.tile(inv_freq, 4), jnp.zeros((64,), f32)]).reshape(1, 128)
    sign_row = jnp.concatenate([jnp.tile(jnp.concatenate([-jnp.ones((half,), f32), jnp.ones((half,), f32)]), 2),
                                jnp.zeros((64,), f32)]).reshape(1, 128)
    cos_t, sin_t = _rope_tables(pos, freq_row, sign_row)

    for l in range(DEPTH):
        wa, wc, wg, wl = _arrange_w_in(w_in[l])
        qkv_a, qkv_c, gates, lat = _inproj(xf, wa, wc, wg, wl, b_gate[l].reshape(1, 3 * D_MODEL))

        wq1, wq2 = _arrange_w_uq(b_w_uq[l])
        wk, wv = _arrange_w_ukv(b_w_ukv[l])
        q_b, k_b, v_b = _latent(lat, cos_t, sin_t,
                                b_q_norm[l].reshape(1, B_Q_LORA), b_kv_norm[l].reshape(1, B_KV_LORA),
                                wq1, wq2, wk, wv)

        cf_t = lat[:, 704:712].reshape(BATCH, SEQ, HEADS).transpose(0, 2, 1).reshape(BATCH * HEADS, SEQ)
        bf_col = jnp.tile(b_forget[l], BATCH).reshape(BATCH * HEADS, 1)
        forget = _forget_cumsum(cf_t, bf_col).reshape(BATCH, PAIRS, 2, 1, SEQ)

        rel_table = _rel_table(_rel_bias_rows(a_rel_bias[l]))
        y_a = _attention(qkv_a, qkv_a, qkv_a, rel_table, mode="rel", dq=PAIR_W, tq=ATT_T, pps=4,
                         kcol0=PAIRS, vcol0=2 * PAIRS)
        y_b = _attention(q_b, k_b, v_b, None, mode="mla", dq=B_PAIR_W, tq=CAUSAL_T, pps=2, kcol0=0, vcol0=0)
        y_c = _attention(qkv_c, qkv_c, qkv_c, forget, mode="fox", dq=PAIR_W, tq=CAUSAL_T, pps=2,
                         kcol0=PAIRS, vcol0=2 * PAIRS)

        xf = _merge(xf, y_a, y_b, y_c, gates, w_branch[l].astype(bf16), w_mix_out[l].astype(bf16),
                    ln_mix_g[l].reshape(1, D_MODEL), ln_mix_b[l].reshape(1, D_MODEL))
        xf = _xattn(xf, memf, xa_w_q[l].astype(bf16), xa_w_kv[l].astype(bf16), xa_w_o[l].astype(bf16),
                    ln_xa_g[l].reshape(1, D_MODEL), ln_xa_b[l].reshape(1, D_MODEL))
        xf = _ffn(xf, ffn_w_gu[l].astype(bf16), ffn_w_down[l].astype(bf16),
                  ln_ffn_g[l].reshape(1, D_MODEL), ln_ffn_b[l].reshape(1, D_MODEL))
    return xf.reshape(BATCH, SEQ, D_MODEL)
```

```python
import functools
import math

import jax
import jax.numpy as jnp
from jax import lax
from jax.experimental import pallas as pl
from jax.experimental.pallas import tpu as pltpu

D_MODEL = 1024
BATCH = 8
SEQ = 2048
DEPTH = 2
TOKENS = BATCH * SEQ
CHUNK = 64
MEM_LEN = 256

HEADS = 8
HEAD_DIM = 64
PAIRS = HEADS // 2
PAIR_W = 2 * HEAD_DIM
BRANCH_W = HEADS * HEAD_DIM

A_LEFT_CHUNKS = 8
ATT_T = 256
A_WINDOW = 3 * ATT_T
CAUSAL_T = 512

B_Q_LORA = 384
B_KV_LORA = 256
B_NOPE = 64
B_ROPE = 32
B_PAIR_W = 256
ROPE_BASE = 10000.0

XA_HEADS = 4
XA_HEAD_DIM = 128
FFN_HIDDEN = 2816
FFN_CHUNK = 256

LN_EPS = 1e-5
RMS_EPS = 1e-6
ALPHA = (2 * DEPTH) ** 0.25
LOG2E = math.log2(math.e)

LAT_W = 896
CF_ROW0 = 64

VMEM_LIMIT = 56 * 1024 * 1024

bf16 = jnp.bfloat16
f32 = jnp.float32


def _dot(a, b):
    return jnp.dot(a, b, preferred_element_type=f32)


def _dot_t(a, b):
    return lax.dot_general(a, b, (((1,), (1,)), ((), ())), preferred_element_type=f32)


def _layer_norm(z, g, b):
    mu = jnp.mean(z, axis=-1, keepdims=True)
    zc = z - mu
    var = jnp.mean(zc * zc, axis=-1, keepdims=True)
    return zc * lax.rsqrt(var + LN_EPS) * g + b


def _rms_norm(z, g):
    ms = jnp.mean(z * z, axis=-1, keepdims=True)
    return z * lax.rsqrt(ms + RMS_EPS) * g


def _params(*sem):
    return pltpu.CompilerParams(dimension_semantics=sem, vmem_limit_bytes=VMEM_LIMIT)


def _resident(shape):
    zeros = (0,) * len(shape)
    return pl.BlockSpec(shape, lambda *_: zeros, pipeline_mode=pl.Buffered(1))


def _layer(shape, l):
    zeros = (0,) * len(shape)
    return pl.BlockSpec((None,) + tuple(shape), lambda *_: (l,) + zeros, pipeline_mode=pl.Buffered(1))


def _inproj_kernel(x_ref, wa_ref, wc_ref, wg_ref, wl_ref, bg_ref,
                   qkva_ref, qkvc_ref, gate_ref, lat_ref, cft_ref):
    xb = x_ref[...].astype(bf16)
    qk_scale = HEAD_DIM ** -0.5 * LOG2E

    for dst, w_ref in ((qkva_ref, wa_ref), (qkvc_ref, wc_ref)):
        dst[:, 0:512] = (_dot(xb, w_ref[:, 0:512]) * qk_scale).astype(bf16)
        dst[:, 512:1024] = _dot(xb, w_ref[:, 512:1024]).astype(bf16)
        dst[:, 1024:1536] = _dot(xb, w_ref[:, 1024:1536]).astype(bf16)
    for c in range(6):
        z = _dot(xb, wg_ref[:, c * 512:(c + 1) * 512]) + bg_ref[:, c * 512:(c + 1) * 512]
        gate_ref[:, c * 512:(c + 1) * 512] = (1.0 / (1.0 + jnp.exp(-z))).astype(bf16)
    lat_ref[:, 0:384] = _dot(xb, wl_ref[:, 0:384])
    lat_ref[:, 384:640] = _dot(xb, wl_ref[:, 384:640])
    kr_cf = _dot(xb, wl_ref[:, 640:768])
    lat_ref[:, 640:768] = kr_cf
    cft_ref[...] = kr_cf.T
    lat_ref[:, 768:896] = _dot(xb, wl_ref[:, 768:896])


def _inproj(x, wa, wc, wg, wl, bg, l):
    tm = 512
    row = lambda i: (i, 0)
    return pl.pallas_call(
        _inproj_kernel,
        grid=(TOKENS // tm,),
        in_specs=[
            pl.BlockSpec((tm, D_MODEL), row),
            _layer((D_MODEL, 1536), l),
            _layer((D_MODEL, 1536), l),
            _layer((D_MODEL, 3 * D_MODEL), l),
            _layer((D_MODEL, LAT_W), l),
            _layer((1, 3 * D_MODEL), l),
        ],
        out_specs=[
            pl.BlockSpec((tm, 1536), row),
            pl.BlockSpec((tm, 1536), row),
            pl.BlockSpec((tm, 3 * D_MODEL), row),
            pl.BlockSpec((tm, LAT_W), row),
            pl.BlockSpec((128, tm), lambda i: (0, i)),
        ],
        out_shape=[
            jax.ShapeDtypeStruct((TOKENS, 1536), bf16),
            jax.ShapeDtypeStruct((TOKENS, 1536), bf16),
            jax.ShapeDtypeStruct((TOKENS, 3 * D_MODEL), bf16),
            jax.ShapeDtypeStruct((TOKENS, LAT_W), f32),
            jax.ShapeDtypeStruct((128, TOKENS), f32),
        ],
        compiler_params=_params("parallel"),
        name="inproj",
    )(x, wa, wc, wg, wl, bg)


def _rope_kernel(pos_ref, freq_ref, sign_ref, cos_ref, sin_ref):
    ang = pos_ref[...].astype(f32) * freq_ref[...]
    live = (freq_ref[...] > 0.0).astype(f32)
    cos_ref[...] = jnp.cos(ang) * live
    sin_ref[...] = jnp.sin(ang) * sign_ref[...]


def _rope_tables(pos, freq, sign):
    tm = 2048
    row = lambda i: (i, 0)
    return pl.pallas_call(
        _rope_kernel,
        grid=(TOKENS // tm,),
        in_specs=[pl.BlockSpec((tm, 1), row), _resident((1, 128)), _resident((1, 128))],
        out_specs=[pl.BlockSpec((tm, 128), row), pl.BlockSpec((tm, 128), row)],
        out_shape=[jax.ShapeDtypeStruct((TOKENS, 128), f32), jax.ShapeDtypeStruct((TOKENS, 128), f32)],
        compiler_params=_params("parallel"),
        name="rope_tables",
    )(pos, freq, sign)


def _latent_kernel(lat_ref, cos_ref, sin_ref, gq_ref, gkv_ref,
                   wq1_ref, wq2_ref, wk_ref, wv_ref, qb_ref, kb_ref, vb_ref):
    cos_t = cos_ref[...]
    sin_t = sin_ref[...]
    scale = (B_NOPE + B_ROPE) ** -0.5 * LOG2E

    cqn = _rms_norm(lat_ref[:, 0:384], gq_ref[...]).astype(bf16)
    q1 = _dot(cqn, wq1_ref[...])
    q2 = _dot(cqn, wq2_ref[...])
    ckvn = _rms_norm(lat_ref[:, 384:640], gkv_ref[...]).astype(bf16)
    kn = _dot(ckvn, wk_ref[...])
    vb_ref[...] = _dot(ckvn, wv_ref[...]).astype(bf16)
    kpe = (lat_ref[:, 640:768] * cos_t + lat_ref[:, 768:896] * sin_t).astype(bf16)
    for p in range(PAIRS):
        c0 = p * B_PAIR_W
        qb_ref[:, c0:c0 + 128] = (q1[:, c0:c0 + 128] * scale).astype(bf16)
        pe = q1[:, c0 + 128:c0 + 256] * cos_t + q2[:, p * 128:(p + 1) * 128] * sin_t
        qb_ref[:, c0 + 128:c0 + 256] = (pe * scale).astype(bf16)
        kb_ref[:, c0:c0 + 128] = kn[:, p * 128:(p + 1) * 128].astype(bf16)
        kb_ref[:, c0 + 128:c0 + 256] = kpe


def _latent(lat, cos_t, sin_t, gq, gkv, wq1, wq2, wk, wv, l):
    tm = 512
    row = lambda i: (i, 0)
    return pl.pallas_call(
        _latent_kernel,
        grid=(TOKENS // tm,),
        in_specs=[
            pl.BlockSpec((tm, LAT_W), row),
            pl.BlockSpec((tm, 128), row),
            pl.BlockSpec((tm, 128), row),
            _layer((1, B_Q_LORA), l),
            _layer((1, B_KV_LORA), l),
            _layer((B_Q_LORA, PAIRS * B_PAIR_W), l),
            _layer((B_Q_LORA, PAIRS * 128), l),
            _layer((B_KV_LORA, BRANCH_W), l),
            _layer((B_KV_LORA, BRANCH_W), l),
        ],
        out_specs=[
            pl.BlockSpec((tm, PAIRS * B_PAIR_W), row),
            pl.BlockSpec((tm, PAIRS * B_PAIR_W), row),
            pl.BlockSpec((tm, BRANCH_W), row),
        ],
        out_shape=[
            jax.ShapeDtypeStruct((TOKENS, PAIRS * B_PAIR_W), bf16),
            jax.ShapeDtypeStruct((TOKENS, PAIRS * B_PAIR_W), bf16),
            jax.ShapeDtypeStruct((TOKENS, BRANCH_W), bf16),
        ],
        compiler_params=_params("parallel"),
        name="latent",
    )(lat, cos_t, sin_t, gq, gkv, wq1, wq2, wk, wv)


def _forget_kernel(cf_ref, bf_ref, o_ref):
    lane = lax.broadcasted_iota(jnp.int32, (HEADS, 128), 1)
    carry = None
    for blk in range(TOKENS // 128):
        z = cf_ref[:, blk * 128:(blk + 1) * 128] + bf_ref[...]
        acc = jnp.minimum(z, 0.0) - jnp.log(1.0 + jnp.exp(-jnp.abs(z)))
        d = 1
        while d < 128:
            acc = acc + jnp.where(lane >= d, pltpu.roll(acc, d, 1), 0.0)
            d *= 2
        if blk % (SEQ // 128) != 0:
            acc = acc + carry
        o_ref[:, blk * 128:(blk + 1) * 128] = acc
        carry = acc[:, 127:128]


def _forget_cumsum(cf_t, bf, l):
    return pl.pallas_call(
        _forget_kernel,
        grid=(1,),
        in_specs=[
            pl.BlockSpec((HEADS, TOKENS), lambda i: (CF_ROW0 // HEADS, 0)),
            _layer((HEADS, 1), l),
        ],
        out_specs=pl.BlockSpec((HEADS, TOKENS), lambda i: (0, 0)),
        out_shape=jax.ShapeDtypeStruct((HEADS, TOKENS), f32),
        compiler_params=_params("arbitrary"),
        name="forget_cumsum",
    )(cf_t, bf)


def _rel_table_kernel(w_ref, o_ref):
    x = jnp.broadcast_to(w_ref[...], (ATT_T, 1024))
    row = lax.broadcasted_iota(jnp.int32, (ATT_T, 1024), 0)
    x = pltpu.roll(x, 1024 - ATT_T, 1)
    for bit in range(8):
        x = jnp.where(((row >> bit) & 1) == 1, pltpu.roll(x, 1 << bit, 1), x)
    r = lax.broadcasted_iota(jnp.int32, (ATT_T, A_WINDOW), 0)
    c = lax.broadcasted_iota(jnp.int32, (ATT_T, A_WINDOW), 1)
    dchunk = (c >> 6) - (r >> 6)
    valid = (dchunk >= 0) & (dchunk <= A_LEFT_CHUNKS)
    o_ref[...] = jnp.where(valid, x[:, 0:A_WINDOW] * LOG2E, -jnp.inf)


def _rel_table(w_rows, l):
    return pl.pallas_call(
        _rel_table_kernel,
        grid=(HEADS,),
        in_specs=[pl.BlockSpec((None, None, 1, 1024), lambda h: (l, h, 0, 0))],
        out_specs=pl.BlockSpec((None, ATT_T, A_WINDOW), lambda h: (h, 0, 0)),
        out_shape=jax.ShapeDtypeStruct((HEADS, ATT_T, A_WINDOW), f32),
        name="rel_table",
    )(w_rows)


def _pair_select(x, width):
    lane = lax.broadcasted_iota(jnp.int32, (1, width), 1)
    if width == PAIR_W:
        in_a = lane < HEAD_DIM
        in_b = lane >= HEAD_DIM
    else:
        in_a = (lane < 64) | ((lane >= 128) & (lane < 160))
        in_b = ((lane >= 64) & (lane < 128)) | ((lane >= 160) & (lane < 192))
    zero = jnp.zeros_like(x)
    return jnp.where(in_a, x, zero), jnp.where(in_b, x, zero)


def _attn_kernel(*refs, mode, dq, tq, pps):
    if mode == "mla":
        q_ref, k_ref, v_ref, o_ref = refs
        extra_ref = None
    else:
        q_ref, k_ref, v_ref, extra_ref, o_ref = refs
    i = pl.program_id(2)
    nq = SEQ // tq
    neg = jnp.float32(-jnp.inf)
    stack_pv = 2 * tq <= 512

    def attend(k0, nkeys, q0):
        lane = lax.broadcasted_iota(jnp.int32, (1, PAIR_W), 1)
        for p in range(pps):
            qa, qb = _pair_select(q_ref[:, p * dq:(p + 1) * dq], dq)
            kt = k_ref[pl.ds(k0, nkeys), p * dq:(p + 1) * dq]
            vt = v_ref[pl.ds(k0, nkeys), p * PAIR_W:(p + 1) * PAIR_W]
            s2 = _dot_t(jnp.concatenate([qa, qb], axis=0), kt)
            probs = []
            for head in range(2):
                s = s2[head * tq:(head + 1) * tq]
                if mode == "rel":
                    s = s + extra_ref[2 * p + head, :, A_WINDOW - nkeys:A_WINDOW]
                elif mode == "fox":
                    f = extra_ref[2 * p + head:2 * p + head + 1, :]
                    s = s + (f[:, q0:q0 + 1] - f[:, 0:nkeys]) * LOG2E
                if mode != "rel":
                    row = lax.broadcasted_iota(jnp.int32, (tq, tq), 0)
                    col = lax.broadcasted_iota(jnp.int32, (tq, tq), 1)
                    keep = (col <= row) if mode == "fox" else ((col >> 6) <= (row >> 6))
                    tail = jnp.where(keep, s[:, nkeys - tq:], neg)
                    s = tail if nkeys == tq else jnp.concatenate([s[:, :nkeys - tq], tail], axis=1)
                m = jnp.max(s, axis=-1, keepdims=True)
                probs.append(jnp.exp2(s - m).astype(bf16))
            v_ones = jnp.concatenate([vt, jnp.ones_like(vt)], axis=1)
            if stack_pv:
                o2 = _dot(jnp.concatenate(probs, axis=0), v_ones)
                oa, ob = o2[0:tq], o2[tq:]
            else:
                oa, ob = _dot(probs[0], v_ones), _dot(probs[1], v_ones)
            oa = oa[:, 0:PAIR_W] / oa[:, PAIR_W:]
            ob = ob[:, 0:PAIR_W] / ob[:, PAIR_W:]
            o_ref[:, p * PAIR_W:(p + 1) * PAIR_W] = jnp.where(lane < HEAD_DIM, oa, ob).astype(bf16)

    if mode == "rel":
        for ii in range(2):
            pl.when(i == ii)(functools.partial(attend, 0, (ii + 1) * tq, 0))

        @pl.when(i >= 2)
        def _():
            attend(pl.multiple_of((i - 2) * tq, tq), A_WINDOW, 0)
    else:
        for ii in range(nq):
            pl.when(i == ii)(functools.partial(attend, 0, (ii + 1) * tq, ii * tq))


def _attention(q, k, v, extra, *, mode, dq, tq, pps, kcol0, vcol0):
    nq = SEQ // tq
    groups = PAIRS // pps
    in_specs = [
        pl.BlockSpec((tq, pps * dq), lambda b, j, i: (b * nq + i, j)),
        pl.BlockSpec((SEQ, pps * dq), lambda b, j, i: (b, kcol0 // pps + j)),
        pl.BlockSpec((SEQ, pps * PAIR_W), lambda b, j, i: (b, vcol0 // pps + j)),
    ]
    args = [q, k, v]
    if mode == "fox":
        in_specs.append(pl.BlockSpec((None, 2 * pps, SEQ), lambda b, j, i: (j, 0, b)))
        args.append(extra.reshape(groups, 2 * pps, TOKENS))
    elif mode == "rel":
        assert tq == ATT_T
        in_specs.append(pl.BlockSpec((2 * pps, ATT_T, A_WINDOW), lambda b, j, i: (j, 0, 0)))
        args.append(extra)
    return pl.pallas_call(
        functools.partial(_attn_kernel, mode=mode, dq=dq, tq=tq, pps=pps),
        grid=(BATCH, groups, nq),
        in_specs=in_specs,
        out_specs=pl.BlockSpec((tq, pps * PAIR_W), lambda b, j, i: (b * nq + i, j)),
        out_shape=jax.ShapeDtypeStruct((TOKENS, BRANCH_W), bf16),
        compiler_params=_params("parallel", "parallel", "arbitrary"),
        name="attn_" + mode,
    )(*args)


def _merge_kernel(x_ref, ya_ref, yb_ref, yc_ref, gate_ref, wb_ref, wo_ref, g_ref, b_ref, o_ref):
    merged = None
    for n, y_ref in enumerate((ya_ref, yb_ref, yc_ref)):
        proj = _dot(y_ref[...], wb_ref[n])
        term = gate_ref[:, n * D_MODEL:(n + 1) * D_MODEL].astype(f32) * proj
        merged = term if merged is None else merged + term
    y = _dot(merged.astype(bf16), wo_ref[...])
    o_ref[...] = _layer_norm(ALPHA * x_ref[...] + y, g_ref[...], b_ref[...])


def _merge(x, ya, yb, yc, gates, wb, wo, g, b, l):
    tm = 512
    row = lambda i: (i, 0)
    return pl.pallas_call(
        _merge_kernel,
        grid=(TOKENS // tm,),
        in_specs=[
            pl.BlockSpec((tm, D_MODEL), row),
            pl.BlockSpec((tm, BRANCH_W), row),
            pl.BlockSpec((tm, BRANCH_W), row),
            pl.BlockSpec((tm, BRANCH_W), row),
            pl.BlockSpec((tm, 3 * D_MODEL), row),
            _layer((3, BRANCH_W, D_MODEL), l),
            _layer((D_MODEL, D_MODEL), l),
            _layer((1, D_MODEL), l),
            _layer((1, D_MODEL), l),
        ],
        out_specs=pl.BlockSpec((tm, D_MODEL), row),
        out_shape=jax.ShapeDtypeStruct((TOKENS, D_MODEL), f32),
        compiler_params=_params("parallel"),
        name="merge",
    )(x, ya, yb, yc, gates, wb, wo, g, b)


def _xattn_kernel(x_ref, mem_ref, wq_ref, wkv_ref, wo_ref, g_ref, b_ref, o_ref, kv_ref):
    @pl.when(pl.program_id(1) == 0)
    def _():
        kv_ref[...] = _dot(mem_ref[...].astype(bf16), wkv_ref[...]).astype(bf16)

    x = x_ref[...]
    scale = XA_HEAD_DIM ** -0.5 * LOG2E
    q = (_dot(x.astype(bf16), wq_ref[...]) * scale).astype(bf16)
    width = XA_HEADS * XA_HEAD_DIM
    outs = []
    for h in range(XA_HEADS):
        c0 = h * XA_HEAD_DIM
        s = _dot_t(q[:, c0:c0 + XA_HEAD_DIM], kv_ref[:, c0:c0 + XA_HEAD_DIM])
        p = jnp.exp2(s - jnp.max(s, axis=-1, keepdims=True))
        l = jnp.sum(p, axis=-1, keepdims=True)
        o = _dot(p.astype(bf16), kv_ref[:, width + c0:width + c0 + XA_HEAD_DIM])
        outs.append((o / l).astype(bf16))
    y = _dot(jnp.concatenate(outs, axis=-1), wo_ref[...])
    o_ref[...] = _layer_norm(ALPHA * x + y, g_ref[...], b_ref[...])


def _xattn(x, mem, wq, wkv, wo, g, b, l):
    tm = 512
    nq = SEQ // tm
    width = XA_HEADS * XA_HEAD_DIM
    return pl.pallas_call(
        _xattn_kernel,
        grid=(BATCH, nq),
        in_specs=[
            pl.BlockSpec((tm, D_MODEL), lambda bi, i: (bi * nq + i, 0)),
            pl.BlockSpec((MEM_LEN, D_MODEL), lambda bi, i: (bi, 0)),
            _layer((D_MODEL, width), l),
            _layer((D_MODEL, 2 * width), l),
            _layer((width, D_MODEL), l),
            _layer((1, D_MODEL), l),
            _layer((1, D_MODEL), l),
        ],
        out_specs=pl.BlockSpec((tm, D_MODEL), lambda bi, i: (bi * nq + i, 0)),
        out_shape=jax.ShapeDtypeStruct((TOKENS, D_MODEL), f32),
        scratch_shapes=[pltpu.VMEM((MEM_LEN, 2 * width), bf16)],
        compiler_params=_params("arbitrary", "arbitrary"),
        name="xattn",
    )(x, mem, wq, wkv, wo, g, b)


def _ffn_kernel(x_ref, wgu_ref, wd_ref, g_ref, b_ref, o_ref, acc_ref):
    x = x_ref[...]
    xb = x.astype(bf16)
    for c in range(FFN_HIDDEN // FFN_CHUNK):
        c0 = c * FFN_CHUNK
        gate = _dot(xb, wgu_ref[:, c0:c0 + FFN_CHUNK])
        up = _dot(xb, wgu_ref[:, FFN_HIDDEN + c0:FFN_HIDDEN + c0 + FFN_CHUNK])
        h = (gate / (1.0 + jnp.exp(-gate)) * up).astype(bf16)
        part = _dot(h, wd_ref[c0:c0 + FFN_CHUNK, :])
        if c == 0:
            acc_ref[...] = part
        else:
            acc_ref[...] += part
    o_ref[...] = _layer_norm(ALPHA * x + acc_ref[...], g_ref[...], b_ref[...])


def _ffn(x, wgu, wd, g, b, l):
    tm = 512
    row = lambda i: (i, 0)
    return pl.pallas_call(
        _ffn_kernel,
        grid=(TOKENS // tm,),
        in_specs=[
            pl.BlockSpec((tm, D_MODEL), row),
            _layer((D_MODEL, 2 * FFN_HIDDEN), l),
            _layer((FFN_HIDDEN, D_MODEL), l),
            _layer((1, D_MODEL), l),
            _layer((1, D_MODEL), l),
        ],
        out_specs=pl.BlockSpec((tm, D_MODEL), row),
        out_shape=jax.ShapeDtypeStruct((TOKENS, D_MODEL), f32),
        scratch_shapes=[pltpu.VMEM((tm, D_MODEL), f32)],
        compiler_params=_params("parallel"),
        name="ffn",
    )(x, wgu, wd, g, b)


def _arrange_w_in(w):
    a = w[:, :, 0:1536].astype(bf16)
    c = w[:, :, 2208:3744].astype(bf16)
    g = w[:, :, 3752:6824].astype(bf16)
    small = w[:, :, 1536:2208].astype(bf16)
    cq_ckv = small[:, :, 0:640]
    kr = small[:, :, 640:672]
    cf = w[:, :, 3744:3752].astype(bf16)
    half = B_ROPE // 2
    kr_sw = jnp.concatenate([kr[:, :, half:], kr[:, :, :half]], axis=2)
    lead = w.shape[:2]
    lat = jnp.concatenate([cq_ckv, kr, kr, cf, jnp.zeros(lead + (56,), bf16),
                           kr_sw, kr_sw, jnp.zeros(lead + (64,), bf16)], axis=2)
    return a, c, g, lat


def _arrange_w_uq(w):
    lead = w.shape[:2]
    w = w.reshape(lead + (HEADS, B_NOPE + B_ROPE))
    nope = w[..., :B_NOPE]
    pe = w[..., B_NOPE:]
    half = B_ROPE // 2
    pe_sw = jnp.concatenate([pe[..., half:], pe[..., :half]], axis=-1)
    z64 = jnp.zeros(lead + (64,), w.dtype)
    main, swapped = [], []
    for p in range(PAIRS):
        a, b = 2 * p, 2 * p + 1
        main += [nope[:, :, a], nope[:, :, b], pe[:, :, a], pe[:, :, b], z64]
        swapped += [pe_sw[:, :, a], pe_sw[:, :, b], z64]
    return (jnp.concatenate(main, axis=2).astype(bf16),
            jnp.concatenate(swapped, axis=2).astype(bf16))


def _arrange_w_ukv(w):
    lead = w.shape[:2]
    w = w.reshape(lead + (HEADS, B_NOPE + HEAD_DIM))
    wk = w[..., :B_NOPE].reshape(lead + (BRANCH_W,))
    wv = w[..., B_NOPE:].reshape(lead + (BRANCH_W,))
    return wk.astype(bf16), wv.astype(bf16)


def _rel_bias_rows(rel_bias):
    lead = rel_bias.shape[:2]
    lo = jnp.broadcast_to(rel_bias[..., :1], lead + (1024 - 512 - rel_bias.shape[-1],))
    hi = jnp.broadcast_to(rel_bias[..., -1:], lead + (512,))
    rows = jnp.concatenate([hi, rel_bias[..., ::-1], lo], axis=-1).astype(f32)
    return rows.reshape(lead + (1, 1024))


def kernel(x, mem, positions, ln_mix_g, ln_mix_b, w_in, b_gate, b_forget, a_rel_bias, b_q_norm, b_kv_norm, b_w_uq, b_w_ukv, w_branch, w_mix_out, ln_xa_g, ln_xa_b, xa_w_q, xa_w_kv, xa_w_o, ln_ffn_g, ln_ffn_b, ffn_w_gu, ffn_w_down):
    xf = x.reshape(TOKENS, D_MODEL)
    memf = mem.reshape(BATCH * MEM_LEN, D_MODEL)
    pos = positions.reshape(TOKENS, 1)

    half = B_ROPE // 2
    inv_freq = ROPE_BASE ** (-jnp.arange(half, dtype=f32) / half)
    freq_row = jnp.concatenate([jnp.tile(inv_freq, 4), jnp.zeros((64,), f32)]).reshape(1, 128)
    sign_row = jnp.concatenate([jnp.tile(jnp.concatenate([-jnp.ones((half,), f32), jnp.ones((half,), f32)]), 2),
                                jnp.zeros((64,), f32)]).reshape(1, 128)
    cos_t, sin_t = _rope_tables(pos, freq_row, sign_row)

    vec = lambda p: p.reshape(DEPTH, 1, p.shape[-1])
    wa, wc, wg, wl = _arrange_w_in(w_in)
    wq1, wq2 = _arrange_w_uq(b_w_uq)
    wk, wv = _arrange_w_ukv(b_w_ukv)
    rel_rows = _rel_bias_rows(a_rel_bias)
    w_branch_b, w_out_b = w_branch.astype(bf16), w_mix_out.astype(bf16)
    xa_q, xa_kv, xa_o = xa_w_q.astype(bf16), xa_w_kv.astype(bf16), xa_w_o.astype(bf16)
    w_gu, w_down = ffn_w_gu.astype(bf16), ffn_w_down.astype(bf16)
    bg = b_gate.reshape(DEPTH, 1, 3 * D_MODEL)
    bf = b_forget.reshape(DEPTH, HEADS, 1)

    for l in range(DEPTH):
        qkv_a, qkv_c, gates, lat, cf_t = _inproj(xf, wa, wc, wg, wl, bg, l)
        q_b, k_b, v_b = _latent(lat, cos_t, sin_t, vec(b_q_norm), vec(b_kv_norm), wq1, wq2, wk, wv, l)
        forget = _forget_cumsum(cf_t, bf, l)
        rel_table = _rel_table(rel_rows, l)

        y_a = _attention(qkv_a, qkv_a, qkv_a, rel_table, mode="rel", dq=PAIR_W, tq=ATT_T, pps=4,
                         kcol0=PAIRS, vcol0=2 * PAIRS)
        y_b = _attention(q_b, k_b, v_b, None, mode="mla", dq=B_PAIR_W, tq=CAUSAL_T, pps=2, kcol0=0, vcol0=0)
        y_c = _attention(qkv_c, qkv_c, qkv_c, forget, mode="fox", dq=PAIR_W, tq=CAUSAL_T, pps=2,
                         kcol0=PAIRS, vcol0=2 * PAIRS)

        xf = _merge(xf, y_a, y_b, y_c, gates, w_branch_b, w_out_b, vec(ln_mix_g), vec(ln_mix_b), l)
        xf = _xattn(xf, memf, xa_q, xa_kv, xa_o, vec(ln_xa_g), vec(ln_xa_b), l)
        xf = _ffn(xf, w_gu, w_down, vec(ln_ffn_g), vec(ln_ffn_b), l)
    return xf.reshape(BATCH, SEQ, D_MODEL)
```

```python
import functools
import math

import jax
import jax.numpy as jnp
from jax import lax
from jax.experimental import pallas as pl
from jax.experimental.pallas import tpu as pltpu

D_MODEL = 1024
BATCH = 8
SEQ = 2048
DEPTH = 2
TOKENS = BATCH * SEQ
CHUNK = 64
MEM_LEN = 256

HEADS = 8
HEAD_DIM = 64
PAIRS = HEADS // 2
PAIR_W = 2 * HEAD_DIM
BRANCH_W = HEADS * HEAD_DIM

A_LEFT_CHUNKS = 8
ATT_T = 256
A_WINDOW = 3 * ATT_T
CAUSAL_T = 512

B_Q_LORA = 384
B_KV_LORA = 256
B_NOPE = 64
B_ROPE = 32
B_PAIR_W = 256
ROPE_BASE = 10000.0

XA_HEADS = 4
XA_HEAD_DIM = 128
FFN_HIDDEN = 2816
FFN_CHUNK = 256

LN_EPS = 1e-5
RMS_EPS = 1e-6
ALPHA = (2 * DEPTH) ** 0.25
LOG2E = math.log2(math.e)

LAT_W = 896
CF_ROW0 = 64

ROW_TILE = 1024
SUB_ROWS = 512

VMEM_LIMIT = 56 * 1024 * 1024

bf16 = jnp.bfloat16
f32 = jnp.float32


def _dot(a, b):
    return jnp.dot(a, b, preferred_element_type=f32)


def _dot_t(a, b):
    return lax.dot_general(a, b, (((1,), (1,)), ((), ())), preferred_element_type=f32)


def _layer_norm(z, g, b):
    mu = jnp.mean(z, axis=-1, keepdims=True)
    zc = z - mu
    var = jnp.mean(zc * zc, axis=-1, keepdims=True)
    return zc * lax.rsqrt(var + LN_EPS) * g + b


def _rms_norm(z, g):
    ms = jnp.mean(z * z, axis=-1, keepdims=True)
    return z * lax.rsqrt(ms + RMS_EPS) * g


def _params(*sem):
    return pltpu.CompilerParams(dimension_semantics=sem, vmem_limit_bytes=VMEM_LIMIT)


def _resident(shape):
    zeros = (0,) * len(shape)
    return pl.BlockSpec(shape, lambda *_: zeros, pipeline_mode=pl.Buffered(1))


def _layer(shape, l):
    zeros = (0,) * len(shape)
    return pl.BlockSpec((None,) + tuple(shape), lambda *_: (l,) + zeros, pipeline_mode=pl.Buffered(1))


def _inproj_kernel(x_ref, wa_ref, wc_ref, wg_ref, wl_ref, bg_ref, cos_ref, sin_ref, gq_ref, gkv_ref,
                   wq1_ref, wq2_ref, wk_ref, wv_ref,
                   qkva_ref, qkvc_ref, gate_ref, cft_ref, qb_ref, kb_ref, vb_ref):
    xb = x_ref[...].astype(bf16)
    qk_scale = HEAD_DIM ** -0.5 * LOG2E

    for dst, w_ref in ((qkva_ref, wa_ref), (qkvc_ref, wc_ref)):
        dst[:, 0:512] = (_dot(xb, w_ref[:, 0:512]) * qk_scale).astype(bf16)
        dst[:, 512:1024] = _dot(xb, w_ref[:, 512:1024]).astype(bf16)
        dst[:, 1024:1536] = _dot(xb, w_ref[:, 1024:1536]).astype(bf16)
    for c in range(6):
        z = _dot(xb, wg_ref[:, c * 512:(c + 1) * 512]) + bg_ref[:, c * 512:(c + 1) * 512]
        gate_ref[:, c * 512:(c + 1) * 512] = (1.0 / (1.0 + jnp.exp(-z))).astype(bf16)

    kr_cf = _dot(xb, wl_ref[:, 640:768])
    cft_ref[...] = kr_cf.T
    cos_t = cos_ref[...]
    sin_t = sin_ref[...]
    scale = (B_NOPE + B_ROPE) ** -0.5 * LOG2E
    cqn = _rms_norm(_dot(xb, wl_ref[:, 0:384]), gq_ref[...]).astype(bf16)
    q1 = _dot(cqn, wq1_ref[...])
    q2 = _dot(cqn, wq2_ref[...])
    ckvn = _rms_norm(_dot(xb, wl_ref[:, 384:640]), gkv_ref[...]).astype(bf16)
    kn = _dot(ckvn, wk_ref[...])
    vb_ref[...] = _dot(ckvn, wv_ref[...]).astype(bf16)
    kpe = (kr_cf * cos_t + _dot(xb, wl_ref[:, 768:896]) * sin_t).astype(bf16)
    for p in range(PAIRS):
        c0 = p * B_PAIR_W
        qb_ref[:, c0:c0 + 128] = (q1[:, c0:c0 + 128] * scale).astype(bf16)
        pe = q1[:, c0 + 128:c0 + 256] * cos_t + q2[:, p * 128:(p + 1) * 128] * sin_t
        qb_ref[:, c0 + 128:c0 + 256] = (pe * scale).astype(bf16)
        kb_ref[:, c0:c0 + 128] = kn[:, p * 128:(p + 1) * 128].astype(bf16)
        kb_ref[:, c0 + 128:c0 + 256] = kpe


def _inproj(x, wa, wc, wg, wl, bg, cos_t, sin_t, gq, gkv, wq1, wq2, wk, wv, l):
    tm = 512
    row = lambda i: (i, 0)
    return pl.pallas_call(
        _inproj_kernel,
        grid=(TOKENS // tm,),
        in_specs=[
            pl.BlockSpec((tm, D_MODEL), row),
            _layer((D_MODEL, 1536), l),
            _layer((D_MODEL, 1536), l),
            _layer((D_MODEL, 3 * D_MODEL), l),
            _layer((D_MODEL, LAT_W), l),
            _layer((1, 3 * D_MODEL), l),
            pl.BlockSpec((tm, 128), row),
            pl.BlockSpec((tm, 128), row),
            _layer((1, B_Q_LORA), l),
            _layer((1, B_KV_LORA), l),
            _layer((B_Q_LORA, PAIRS * B_PAIR_W), l),
            _layer((B_Q_LORA, PAIRS * 128), l),
            _layer((B_KV_LORA, BRANCH_W), l),
            _layer((B_KV_LORA, BRANCH_W), l),
        ],
        out_specs=[
            pl.BlockSpec((tm, 1536), row),
            pl.BlockSpec((tm, 1536), row),
            pl.BlockSpec((tm, 3 * D_MODEL), row),
            pl.BlockSpec((128, tm), lambda i: (0, i)),
            pl.BlockSpec((tm, PAIRS * B_PAIR_W), row),
            pl.BlockSpec((tm, PAIRS * B_PAIR_W), row),
            pl.BlockSpec((tm, BRANCH_W), row),
        ],
        out_shape=[
            jax.ShapeDtypeStruct((TOKENS, 1536), bf16),
            jax.ShapeDtypeStruct((TOKENS, 1536), bf16),
            jax.ShapeDtypeStruct((TOKENS, 3 * D_MODEL), bf16),
            jax.ShapeDtypeStruct((128, TOKENS), f32),
            jax.ShapeDtypeStruct((TOKENS, PAIRS * B_PAIR_W), bf16),
            jax.ShapeDtypeStruct((TOKENS, PAIRS * B_PAIR_W), bf16),
            jax.ShapeDtypeStruct((TOKENS, BRANCH_W), bf16),
        ],
        compiler_params=_params("parallel"),
        name="inproj",
    )(x, wa, wc, wg, wl, bg, cos_t, sin_t, gq, gkv, wq1, wq2, wk, wv)


def _rope_kernel(pos_ref, freq_ref, sign_ref, cos_ref, sin_ref):
    ang = pos_ref[...].astype(f32) * freq_ref[...]
    live = (freq_ref[...] > 0.0).astype(f32)
    cos_ref[...] = jnp.cos(ang) * live
    sin_ref[...] = jnp.sin(ang) * sign_ref[...]


def _rope_tables(pos, freq, sign):
    tm = 2048
    row = lambda i: (i, 0)
    return pl.pallas_call(
        _rope_kernel,
        grid=(TOKENS // tm,),
        in_specs=[pl.BlockSpec((tm, 1), row), _resident((1, 128)), _resident((1, 128))],
        out_specs=[pl.BlockSpec((tm, 128), row), pl.BlockSpec((tm, 128), row)],
        out_shape=[jax.ShapeDtypeStruct((TOKENS, 128), f32), jax.ShapeDtypeStruct((TOKENS, 128), f32)],
        compiler_params=_params("parallel"),
        name="rope_tables",
    )(pos, freq, sign)


def _forget_kernel(cf_ref, bf_ref, o_ref):
    lane = lax.broadcasted_iota(jnp.int32, (HEADS, 128), 1)
    carry = None
    for blk in range(TOKENS // 128):
        z = cf_ref[:, blk * 128:(blk + 1) * 128] + bf_ref[...]
        acc = jnp.minimum(z, 0.0) - jnp.log(1.0 + jnp.exp(-jnp.abs(z)))
        d = 1
        while d < 128:
            acc = acc + jnp.where(lane >= d, pltpu.roll(acc, d, 1), 0.0)
            d *= 2
        if blk % (SEQ // 128) != 0:
            acc = acc + carry
        o_ref[:, blk * 128:(blk + 1) * 128] = acc
        carry = acc[:, 127:128]


def _forget_cumsum(cf_t, bf, l):
    return pl.pallas_call(
        _forget_kernel,
        grid=(1,),
        in_specs=[
            pl.BlockSpec((HEADS, TOKENS), lambda i: (CF_ROW0 // HEADS, 0)),
            _layer((HEADS, 1), l),
        ],
        out_specs=pl.BlockSpec((HEADS, TOKENS), lambda i: (0, 0)),
        out_shape=jax.ShapeDtypeStruct((HEADS, TOKENS), f32),
        compiler_params=_params("arbitrary"),
        name="forget_cumsum",
    )(cf_t, bf)


def _rel_table_kernel(w_ref, o_ref):
    x = jnp.broadcast_to(w_ref[...], (ATT_T, 1024))
    row = lax.broadcasted_iota(jnp.int32, (ATT_T, 1024), 0)
    x = pltpu.roll(x, 1024 - ATT_T, 1)
    for bit in range(8):
        x = jnp.where(((row >> bit) & 1) == 1, pltpu.roll(x, 1 << bit, 1), x)
    r = lax.broadcasted_iota(jnp.int32, (ATT_T, A_WINDOW), 0)
    c = lax.broadcasted_iota(jnp.int32, (ATT_T, A_WINDOW), 1)
    dchunk = (c >> 6) - (r >> 6)
    valid = (dchunk >= 0) & (dchunk <= A_LEFT_CHUNKS)
    o_ref[...] = jnp.where(valid, x[:, 0:A_WINDOW] * LOG2E, -jnp.inf)


def _rel_table(w_rows, l):
    return pl.pallas_call(
        _rel_table_kernel,
        grid=(HEADS,),
        in_specs=[pl.BlockSpec((None, None, 1, 1024), lambda h: (l, h, 0, 0))],
        out_specs=pl.BlockSpec((None, ATT_T, A_WINDOW), lambda h: (h, 0, 0)),
        out_shape=jax.ShapeDtypeStruct((HEADS, ATT_T, A_WINDOW), f32),
        name="rel_table",
    )(w_rows)


def _pair_select(x, width):
    lane = lax.broadcasted_iota(jnp.int32, (1, width), 1)
    if width == PAIR_W:
        in_a = lane < HEAD_DIM
        in_b = lane >= HEAD_DIM
    else:
        in_a = (lane < 64) | ((lane >= 128) & (lane < 160))
        in_b = ((lane >= 64) & (lane < 128)) | ((lane >= 160) & (lane < 192))
    zero = jnp.zeros_like(x)
    return jnp.where(in_a, x, zero), jnp.where(in_b, x, zero)


def _attn_kernel(*refs, mode, dq, tq, pps):
    if mode == "mla":
        q_ref, k_ref, v_ref, o_ref = refs
        extra_ref = None
    else:
        q_ref, k_ref, v_ref, extra_ref, o_ref = refs
    i = pl.program_id(2)
    nq = SEQ // tq
    neg = jnp.float32(-jnp.inf)
    stack_pv = 2 * tq <= 512

    def attend(k0, nkeys, q0):
        lane = lax.broadcasted_iota(jnp.int32, (1, PAIR_W), 1)
        for p in range(pps):
            qa, qb = _pair_select(q_ref[:, p * dq:(p + 1) * dq], dq)
            kt = k_ref[pl.ds(k0, nkeys), p * dq:(p + 1) * dq]
            vt = v_ref[pl.ds(k0, nkeys), p * PAIR_W:(p + 1) * PAIR_W]
            s2 = _dot_t(jnp.concatenate([qa, qb], axis=0), kt)
            probs = []
            for head in range(2):
                s = s2[head * tq:(head + 1) * tq]
                if mode == "rel":
                    s = s + extra_ref[2 * p + head, :, A_WINDOW - nkeys:A_WINDOW]
                elif mode == "fox":
                    f = extra_ref[2 * p + head:2 * p + head + 1, :]
                    s = s + (f[:, q0:q0 + 1] - f[:, 0:nkeys]) * LOG2E
                if mode != "rel":
                    row = lax.broadcasted_iota(jnp.int32, (tq, tq), 0)
                    col = lax.broadcasted_iota(jnp.int32, (tq, tq), 1)
                    keep = (col <= row) if mode == "fox" else ((col >> 6) <= (row >> 6))
                    tail = jnp.where(keep, s[:, nkeys - tq:], neg)
                    s = tail if nkeys == tq else jnp.concatenate([s[:, :nkeys - tq], tail], axis=1)
                m = jnp.max(s, axis=-1, keepdims=True)
                probs.append(jnp.exp2(s - m).astype(bf16))
            v_ones = jnp.concatenate([vt, jnp.ones_like(vt)], axis=1)
            if stack_pv:
                o2 = _dot(jnp.concatenate(probs, axis=0), v_ones)
                oa, ob = o2[0:tq], o2[tq:]
            else:
                oa, ob = _dot(probs[0], v_ones), _dot(probs[1], v_ones)
            oa = oa[:, 0:PAIR_W] / oa[:, PAIR_W:]
            ob = ob[:, 0:PAIR_W] / ob[:, PAIR_W:]
            o_ref[:, p * PAIR_W:(p + 1) * PAIR_W] = jnp.where(lane < HEAD_DIM, oa, ob).astype(bf16)

    if mode == "rel":
        for ii in range(2):
            pl.when(i == ii)(functools.partial(attend, 0, (ii + 1) * tq, 0))

        @pl.when(i >= 2)
        def _():
            attend(pl.multiple_of((i - 2) * tq, tq), A_WINDOW, 0)
    else:
        for ii in range(nq):
            pl.when(i == ii)(functools.partial(attend, 0, (ii + 1) * tq, ii * tq))


def _attention(q, k, v, extra, *, mode, dq, tq, pps, kcol0, vcol0):
    nq = SEQ // tq
    groups = PAIRS // pps
    in_specs = [
        pl.BlockSpec((tq, pps * dq), lambda b, j, i: (b * nq + i, j)),
        pl.BlockSpec((SEQ, pps * dq), lambda b, j, i: (b, kcol0 // pps + j)),
        pl.BlockSpec((SEQ, pps * PAIR_W), lambda b, j, i: (b, vcol0 // pps + j)),
    ]
    args = [q, k, v]
    if mode == "fox":
        in_specs.append(pl.BlockSpec((None, 2 * pps, SEQ), lambda b, j, i: (j, 0, b)))
        args.append(extra.reshape(groups, 2 * pps, TOKENS))
    elif mode == "rel":
        assert tq == ATT_T
        in_specs.append(pl.BlockSpec((2 * pps, ATT_T, A_WINDOW), lambda b, j, i: (j, 0, 0)))
        args.append(extra)
    return pl.pallas_call(
        functools.partial(_attn_kernel, mode=mode, dq=dq, tq=tq, pps=pps),
        grid=(BATCH, groups, nq),
        in_specs=in_specs,
        out_specs=pl.BlockSpec((tq, pps * PAIR_W), lambda b, j, i: (b * nq + i, j)),
        out_shape=jax.ShapeDtypeStruct((TOKENS, BRANCH_W), bf16),
        compiler_params=_params("parallel", "parallel", "arbitrary"),
        name="attn_" + mode,
    )(*args)


def _merge_kernel(x_ref, ya_ref, yb_ref, yc_ref, gate_ref, wb_ref, wo_ref, g_ref, b_ref, o_ref):
    for r0 in range(0, x_ref.shape[0], SUB_ROWS):
        rows = slice(r0, r0 + SUB_ROWS)
        merged = None
        for n, y_ref in enumerate((ya_ref, yb_ref, yc_ref)):
            proj = _dot(y_ref[rows, :], wb_ref[n])
            term = gate_ref[rows, n * D_MODEL:(n + 1) * D_MODEL].astype(f32) * proj
            merged = term if merged is None else merged + term
        y = _dot(merged.astype(bf16), wo_ref[...])
        o_ref[rows, :] = _layer_norm(ALPHA * x_ref[rows, :] + y, g_ref[...], b_ref[...])


def _merge(x, ya, yb, yc, gates, wb, wo, g, b, l):
    tm = ROW_TILE
    row = lambda i: (i, 0)
    return pl.pallas_call(
        _merge_kernel,
        grid=(TOKENS // tm,),
        in_specs=[
            pl.BlockSpec((tm, D_MODEL), row),
            pl.BlockSpec((tm, BRANCH_W), row),
            pl.BlockSpec((tm, BRANCH_W), row),
            pl.BlockSpec((tm, BRANCH_W), row),
            pl.BlockSpec((tm, 3 * D_MODEL), row),
            _layer((3, BRANCH_W, D_MODEL), l),
            _layer((D_MODEL, D_MODEL), l),
            _layer((1, D_MODEL), l),
            _layer((1, D_MODEL), l),
        ],
        out_specs=pl.BlockSpec((tm, D_MODEL), row),
        out_shape=jax.ShapeDtypeStruct((TOKENS, D_MODEL), f32),
        compiler_params=_params("parallel"),
        name="merge",
    )(x, ya, yb, yc, gates, wb, wo, g, b)


def _xattn_kernel(x_ref, mem_ref, wq_ref, wkv_ref, wo_ref, g_ref, b_ref, o_ref, kv_ref):
    @pl.when(pl.program_id(1) == 0)
    def _():
        kv_ref[...] = _dot(mem_ref[...].astype(bf16), wkv_ref[...]).astype(bf16)

    scale = XA_HEAD_DIM ** -0.5 * LOG2E
    width = XA_HEADS * XA_HEAD_DIM
    for r0 in range(0, x_ref.shape[0], SUB_ROWS):
        rows = slice(r0, r0 + SUB_ROWS)
        x = x_ref[rows, :]
        q = (_dot(x.astype(bf16), wq_ref[...]) * scale).astype(bf16)
        outs = []
        for h in range(XA_HEADS):
            c0 = h * XA_HEAD_DIM
            s = _dot_t(q[:, c0:c0 + XA_HEAD_DIM], kv_ref[:, c0:c0 + XA_HEAD_DIM])
            p = jnp.exp2(s - jnp.max(s, axis=-1, keepdims=True))
            l = jnp.sum(p, axis=-1, keepdims=True)
            o = _dot(p.astype(bf16), kv_ref[:, width + c0:width + c0 + XA_HEAD_DIM])
            outs.append((o / l).astype(bf16))
        y = _dot(jnp.concatenate(outs, axis=-1), wo_ref[...])
        o_ref[rows, :] = _layer_norm(ALPHA * x + y, g_ref[...], b_ref[...])


def _xattn(x, mem, wq, wkv, wo, g, b, l):
    tm = ROW_TILE
    nq = SEQ // tm
    width = XA_HEADS * XA_HEAD_DIM
    return pl.pallas_call(
        _xattn_kernel,
        grid=(BATCH, nq),
        in_specs=[
            pl.BlockSpec((tm, D_MODEL), lambda bi, i: (bi * nq + i, 0)),
            pl.BlockSpec((MEM_LEN, D_MODEL), lambda bi, i: (bi, 0)),
            _layer((D_MODEL, width), l),
            _layer((D_MODEL, 2 * width), l),
            _layer((width, D_MODEL), l),
            _layer((1, D_MODEL), l),
            _layer((1, D_MODEL), l),
        ],
        out_specs=pl.BlockSpec((tm, D_MODEL), lambda bi, i: (bi * nq + i, 0)),
        out_shape=jax.ShapeDtypeStruct((TOKENS, D_MODEL), f32),
        scratch_shapes=[pltpu.VMEM((MEM_LEN, 2 * width), bf16)],
        compiler_params=_params("arbitrary", "arbitrary"),
        name="xattn",
    )(x, mem, wq, wkv, wo, g, b)


def _ffn_kernel(x_ref, wgu_ref, wd_ref, g_ref, b_ref, o_ref, acc_ref):
    for r0 in range(0, x_ref.shape[0], SUB_ROWS):
        rows = slice(r0, r0 + SUB_ROWS)
        x = x_ref[rows, :]
        xb = x.astype(bf16)
        for c in range(FFN_HIDDEN // FFN_CHUNK):
            c0 = c * FFN_CHUNK
            gate = _dot(xb, wgu_ref[:, c0:c0 + FFN_CHUNK])
            up = _dot(xb, wgu_ref[:, FFN_HIDDEN + c0:FFN_HIDDEN + c0 + FFN_CHUNK])
            h = (gate / (1.0 + jnp.exp(-gate)) * up).astype(bf16)
            part = _dot(h, wd_ref[c0:c0 + FFN_CHUNK, :])
            if c == 0:
                acc_ref[rows, :] = part
            else:
                acc_ref[rows, :] += part
        o_ref[rows, :] = _layer_norm(ALPHA * x + acc_ref[rows, :], g_ref[...], b_ref[...])


def _ffn(x, wgu, wd, g, b, l):
    tm = ROW_TILE
    row = lambda i: (i, 0)
    return pl.pallas_call(
        _ffn_kernel,
        grid=(TOKENS // tm,),
        in_specs=[
            pl.BlockSpec((tm, D_MODEL), row),
            _layer((D_MODEL, 2 * FFN_HIDDEN), l),
            _layer((FFN_HIDDEN, D_MODEL), l),
            _layer((1, D_MODEL), l),
            _layer((1, D_MODEL), l),
        ],
        out_specs=pl.BlockSpec((tm, D_MODEL), row),
        out_shape=jax.ShapeDtypeStruct((TOKENS, D_MODEL), f32),
        scratch_shapes=[pltpu.VMEM((tm, D_MODEL), f32)],
        compiler_params=_params("parallel"),
        name="ffn",
    )(x, wgu, wd, g, b)


def _arrange_w_in(w):
    a = w[:, :, 0:1536].astype(bf16)
    c = w[:, :, 2208:3744].astype(bf16)
    g = w[:, :, 3752:6824].astype(bf16)
    small = w[:, :, 1536:2208].astype(bf16)
    cq_ckv = small[:, :, 0:640]
    kr = small[:, :, 640:672]
    cf = w[:, :, 3744:3752].astype(bf16)
    half = B_ROPE // 2
    kr_sw = jnp.concatenate([kr[:, :, half:], kr[:, :, :half]], axis=2)
    lead = w.shape[:2]
    lat = jnp.concatenate([cq_ckv, kr, kr, cf, jnp.zeros(lead + (56,), bf16),
                           kr_sw, kr_sw, jnp.zeros(lead + (64,), bf16)], axis=2)
    return a, c, g, lat


def _arrange_w_uq(w):
    lead = w.shape[:2]
    w = w.reshape(lead + (HEADS, B_NOPE + B_ROPE))
    nope = w[..., :B_NOPE]
    pe = w[..., B_NOPE:]
    half = B_ROPE // 2
    pe_sw = jnp.concatenate([pe[..., half:], pe[..., :half]], axis=-1)
    z64 = jnp.zeros(lead + (64,), w.dtype)
    main, swapped = [], []
    for p in range(PAIRS):
        a, b = 2 * p, 2 * p + 1
        main += [nope[:, :, a], nope[:, :, b], pe[:, :, a], pe[:, :, b], z64]
        swapped += [pe_sw[:, :, a], pe_sw[:, :, b], z64]
    return (jnp.concatenate(main, axis=2).astype(bf16),
            jnp.concatenate(swapped, axis=2).astype(bf16))


def _arrange_w_ukv(w):
    lead = w.shape[:2]
    w = w.reshape(lead + (HEADS, B_NOPE + HEAD_DIM))
    wk = w[..., :B_NOPE].reshape(lead + (BRANCH_W,))
    wv = w[..., B_NOPE:].reshape(lead + (BRANCH_W,))
    return wk.astype(bf16), wv.astype(bf16)


def _rel_bias_rows(rel_bias):
    lead = rel_bias.shape[:2]
    lo = jnp.broadcast_to(rel_bias[..., :1], lead + (1024 - 512 - rel_bias.shape[-1],))
    hi = jnp.broadcast_to(rel_bias[..., -1:], lead + (512,))
    rows = jnp.concatenate([hi, rel_bias[..., ::-1], lo], axis=-1).astype(f32)
    return rows.reshape(lead + (1, 1024))


def kernel(x, mem, positions, ln_mix_g, ln_mix_b, w_in, b_gate, b_forget, a_rel_bias, b_q_norm, b_kv_norm, b_w_uq, b_w_ukv, w_branch, w_mix_out, ln_xa_g, ln_xa_b, xa_w_q, xa_w_kv, xa_w_o, ln_ffn_g, ln_ffn_b, ffn_w_gu, ffn_w_down):
    xf = x.reshape(TOKENS, D_MODEL)
    memf = mem.reshape(BATCH * MEM_LEN, D_MODEL)
    pos = positions.reshape(TOKENS, 1)

    half = B_ROPE // 2
    inv_freq = ROPE_BASE ** (-jnp.arange(half, dtype=f32) / half)
    freq_row = jnp.concatenate([jnp.tile(inv_freq, 4), jnp.zeros((64,), f32)]).reshape(1, 128)
    sign_row = jnp.concatenate([jnp.tile(jnp.concatenate([-jnp.ones((half,), f32), jnp.ones((half,), f32)]), 2),
                                jnp.zeros((64,), f32)]).reshape(1, 128)
    cos_t, sin_t = _rope_tables(pos, freq_row, sign_row)

    vec = lambda p: p.reshape(DEPTH, 1, p.shape[-1])
    wa, wc, wg, wl = _arrange_w_in(w_in)
    wq1, wq2 = _arrange_w_uq(b_w_uq)
    wk, wv = _arrange_w_ukv(b_w_ukv)
    rel_rows = _rel_bias_rows(a_rel_bias)
    w_branch_b, w_out_b = w_branch.astype(bf16), w_mix_out.astype(bf16)
    xa_q, xa_kv, xa_o = xa_w_q.astype(bf16), xa_w_kv.astype(bf16), xa_w_o.astype(bf16)
    w_gu, w_down = ffn_w_gu.astype(bf16), ffn_w_down.astype(bf16)
    bg = b_gate.reshape(DEPTH, 1, 3 * D_MODEL)
    bf = b_forget.reshape(DEPTH, HEADS, 1)

    for l in range(DEPTH):
        qkv_a, qkv_c, gates, cf_t, q_b, k_b, v_b = _inproj(
            xf, wa, wc, wg, wl, bg, cos_t, sin_t, vec(b_q_norm), vec(b_kv_norm), wq1, wq2, wk, wv, l)
        forget = _forget_cumsum(cf_t, bf, l)
        rel_table = _rel_table(rel_rows, l)

        y_a = _attention(qkv_a, qkv_a, qkv_a, rel_table, mode="rel", dq=PAIR_W, tq=ATT_T, pps=4,
                         kcol0=PAIRS, vcol0=2 * PAIRS)
        y_b = _attention(q_b, k_b, v_b, None, mode="mla", dq=B_PAIR_W, tq=CAUSAL_T, pps=4, kcol0=0, vcol0=0)
        y_c = _attention(qkv_c, qkv_c, qkv_c, forget, mode="fox", dq=PAIR_W, tq=CAUSAL_T, pps=4,
                         kcol0=PAIRS, vcol0=2 * PAIRS)

        xf = _merge(xf, y_a, y_b, y_c, gates, w_branch_b, w_out_b, vec(ln_mix_g), vec(ln_mix_b), l)
        xf = _xattn(xf, memf, xa_q, xa_kv, xa_o, vec(ln_xa_g), vec(ln_xa_b), l)
        xf = _ffn(xf, w_gu, w_down, vec(ln_ffn_g), vec(ln_ffn_b), l)
    return xf.reshape(BATCH, SEQ, D_MODEL)
```

```python
import functools
import math

import jax
import jax.numpy as jnp
from jax import lax
from jax.experimental import pallas as pl
from jax.experimental.pallas import tpu as pltpu

D_MODEL = 1024
BATCH = 8
SEQ = 2048
DEPTH = 2
TOKENS = BATCH * SEQ
CHUNK = 64
MEM_LEN = 256

HEADS = 8
HEAD_DIM = 64
PAIRS = HEADS // 2
PAIR_W = 2 * HEAD_DIM
BRANCH_W = HEADS * HEAD_DIM

A_LEFT_CHUNKS = 8
ATT_T = 256
A_WINDOW = 3 * ATT_T
CAUSAL_T = 512

B_Q_LORA = 384
B_KV_LORA = 256
B_NOPE = 64
B_ROPE = 32
B_PAIR_W = 256
ROPE_BASE = 10000.0

XA_HEADS = 4
XA_HEAD_DIM = 128
FFN_HIDDEN = 2816
FFN_CHUNK = 256

LN_EPS = 1e-5
RMS_EPS = 1e-6
ALPHA = (2 * DEPTH) ** 0.25
LOG2E = math.log2(math.e)

LAT_W = 896
CF_ROW0 = 64

ROW_TILE = 1024
SUB_ROWS = 512

VMEM_LIMIT = 56 * 1024 * 1024

bf16 = jnp.bfloat16
f32 = jnp.float32


def _dot(a, b):
    return jnp.dot(a, b, preferred_element_type=f32)


def _dot_t(a, b):
    return lax.dot_general(a, b, (((1,), (1,)), ((), ())), preferred_element_type=f32)


def _layer_norm(z, g, b):
    mu = jnp.mean(z, axis=-1, keepdims=True)
    zc = z - mu
    var = jnp.mean(zc * zc, axis=-1, keepdims=True)
    return zc * lax.rsqrt(var + LN_EPS) * g + b


def _rms_norm(z, g):
    ms = jnp.mean(z * z, axis=-1, keepdims=True)
    return z * lax.rsqrt(ms + RMS_EPS) * g


def _params(*sem):
    return pltpu.CompilerParams(dimension_semantics=sem, vmem_limit_bytes=VMEM_LIMIT)


def _resident(shape):
    zeros = (0,) * len(shape)
    return pl.BlockSpec(shape, lambda *_: zeros, pipeline_mode=pl.Buffered(1))


def _layer(shape, l):
    zeros = (0,) * len(shape)
    return pl.BlockSpec((None,) + tuple(shape), lambda *_: (l,) + zeros, pipeline_mode=pl.Buffered(1))


def _inproj_kernel(x_ref, wa_ref, wc_ref, wg_ref, wl_ref, bg_ref, cos_ref, sin_ref, gq_ref, gkv_ref,
                   wq_ref, wk_ref, wv_ref,
                   qkva_ref, qkvc_ref, gate_ref, cft_ref, qb_ref, kb_ref, vb_ref):
    xb = x_ref[...].astype(bf16)
    qk_scale = HEAD_DIM ** -0.5 * LOG2E

    for dst, w_ref in ((qkva_ref, wa_ref), (qkvc_ref, wc_ref)):
        dst[:, 0:512] = (_dot(xb, w_ref[:, 0:512]) * qk_scale).astype(bf16)
        dst[:, 512:1024] = _dot(xb, w_ref[:, 512:1024]).astype(bf16)
        dst[:, 1024:1536] = _dot(xb, w_ref[:, 1024:1536]).astype(bf16)
    for c in range(6):
        z = _dot(xb, wg_ref[:, c * 512:(c + 1) * 512]) + bg_ref[:, c * 512:(c + 1) * 512]
        gate_ref[:, c * 512:(c + 1) * 512] = (1.0 / (1.0 + jnp.exp(-z))).astype(bf16)

    kr_cf = _dot(xb, wl_ref[:, 640:768])
    cft_ref[...] = kr_cf.T
    cos_t = cos_ref[...]
    sin_t = sin_ref[...]
    scale = (B_NOPE + B_ROPE) ** -0.5 * LOG2E
    cqn = _rms_norm(_dot(xb, wl_ref[:, 0:384]), gq_ref[...]).astype(bf16)
    q1 = _dot(cqn, wq_ref[...])
    ckvn = _rms_norm(_dot(xb, wl_ref[:, 384:640]), gkv_ref[...]).astype(bf16)
    kn = _dot(ckvn, wk_ref[...])
    vb_ref[...] = _dot(ckvn, wv_ref[...]).astype(bf16)
    kpe = (kr_cf * cos_t + _dot(xb, wl_ref[:, 768:896]) * sin_t).astype(bf16)
    for p in range(PAIRS):
        c0 = p * B_PAIR_W
        qb_ref[:, c0:c0 + 128] = (q1[:, c0:c0 + 128] * scale).astype(bf16)
        blk = q1[:, c0 + 128:c0 + 256]
        pe = blk * cos_t + pltpu.roll(blk, 64, 1) * sin_t
        qb_ref[:, c0 + 128:c0 + 256] = (pe * scale).astype(bf16)
        kb_ref[:, c0:c0 + 128] = kn[:, p * 128:(p + 1) * 128].astype(bf16)
        kb_ref[:, c0 + 128:c0 + 256] = kpe


def _inproj(x, wa, wc, wg, wl, bg, cos_t, sin_t, gq, gkv, wq, wk, wv, l):
    tm = 512
    row = lambda i: (i, 0)
    return pl.pallas_call(
        _inproj_kernel,
        grid=(TOKENS // tm,),
        in_specs=[
            pl.BlockSpec((tm, D_MODEL), row),
            _layer((D_MODEL, 1536), l),
            _layer((D_MODEL, 1536), l),
            _layer((D_MODEL, 3 * D_MODEL), l),
            _layer((D_MODEL, LAT_W), l),
            _layer((1, 3 * D_MODEL), l),
            pl.BlockSpec((tm, 128), row),
            pl.BlockSpec((tm, 128), row),
            _layer((1, B_Q_LORA), l),
            _layer((1, B_KV_LORA), l),
            _layer((B_Q_LORA, PAIRS * B_PAIR_W), l),
            _layer((B_KV_LORA, BRANCH_W), l),
            _layer((B_KV_LORA, BRANCH_W), l),
        ],
        out_specs=[
            pl.BlockSpec((tm, 1536), row),
            pl.BlockSpec((tm, 1536), row),
            pl.BlockSpec((tm, 3 * D_MODEL), row),
            pl.BlockSpec((128, tm), lambda i: (0, i)),
            pl.BlockSpec((tm, PAIRS * B_PAIR_W), row),
            pl.BlockSpec((tm, PAIRS * B_PAIR_W), row),
            pl.BlockSpec((tm, BRANCH_W), row),
        ],
        out_shape=[
            jax.ShapeDtypeStruct((TOKENS, 1536), bf16),
            jax.ShapeDtypeStruct((TOKENS, 1536), bf16),
            jax.ShapeDtypeStruct((TOKENS, 3 * D_MODEL), bf16),
            jax.ShapeDtypeStruct((128, TOKENS), f32),
            jax.ShapeDtypeStruct((TOKENS, PAIRS * B_PAIR_W), bf16),
            jax.ShapeDtypeStruct((TOKENS, PAIRS * B_PAIR_W), bf16),
            jax.ShapeDtypeStruct((TOKENS, BRANCH_W), bf16),
        ],
        compiler_params=_params("parallel"),
        name="inproj",
    )(x, wa, wc, wg, wl, bg, cos_t, sin_t, gq, gkv, wq, wk, wv)


def _rope_kernel(pos_ref, freq_ref, sign_ref, cos_ref, sin_ref):
    ang = pos_ref[...].astype(f32) * freq_ref[...]
    live = (freq_ref[...] > 0.0).astype(f32)
    cos_ref[...] = jnp.cos(ang) * live
    sin_ref[...] = jnp.sin(ang) * sign_ref[...]


def _rope_tables(pos, freq, sign):
    tm = 2048
    row = lambda i: (i, 0)
    return pl.pallas_call(
        _rope_kernel,
        grid=(TOKENS // tm,),
        in_specs=[pl.BlockSpec((tm, 1), row), _resident((1, 128)), _resident((1, 128))],
        out_specs=[pl.BlockSpec((tm, 128), row), pl.BlockSpec((tm, 128), row)],
        out_shape=[jax.ShapeDtypeStruct((TOKENS, 128), f32), jax.ShapeDtypeStruct((TOKENS, 128), f32)],
        compiler_params=_params("parallel"),
        name="rope_tables",
    )(pos, freq, sign)


def _forget_kernel(cf_ref, bf_ref, o_ref):
    rows = BATCH * HEADS
    lane = lax.broadcasted_iota(jnp.int32, (rows, 128), 1)
    bias = jnp.concatenate([bf_ref[...]] * BATCH, axis=0)
    carry = None
    for blk in range(SEQ // 128):
        z = jnp.concatenate([cf_ref[:, b * SEQ + blk * 128:b * SEQ + (blk + 1) * 128]
                             for b in range(BATCH)], axis=0) + bias
        acc = jnp.minimum(z, 0.0) - jnp.log(1.0 + jnp.exp(-jnp.abs(z)))
        d = 1
        while d < 128:
            acc = acc + jnp.where(lane >= d, pltpu.roll(acc, d, 1), 0.0)
            d *= 2
        if blk > 0:
            acc = acc + carry
        for b in range(BATCH):
            o_ref[:, b * SEQ + blk * 128:b * SEQ + (blk + 1) * 128] = acc[b * HEADS:(b + 1) * HEADS]
        carry = acc[:, 127:128]


def _forget_cumsum(cf_t, bf, l):
    return pl.pallas_call(
        _forget_kernel,
        grid=(1,),
        in_specs=[
            pl.BlockSpec((HEADS, TOKENS), lambda i: (CF_ROW0 // HEADS, 0)),
            _layer((HEADS, 1), l),
        ],
        out_specs=pl.BlockSpec((HEADS, TOKENS), lambda i: (0, 0)),
        out_shape=jax.ShapeDtypeStruct((HEADS, TOKENS), f32),
        compiler_params=_params("arbitrary"),
        name="forget_cumsum",
    )(cf_t, bf)


def _rel_table_kernel(w_ref, o_ref):
    x = jnp.broadcast_to(w_ref[...], (ATT_T, 1024))
    x = pltpu.roll(x, 1024 - ATT_T, 1, stride=1, stride_axis=0)
    r = lax.broadcasted_iota(jnp.int32, (ATT_T, A_WINDOW), 0)
    c = lax.broadcasted_iota(jnp.int32, (ATT_T, A_WINDOW), 1)
    dchunk = (c >> 6) - (r >> 6)
    valid = (dchunk >= 0) & (dchunk <= A_LEFT_CHUNKS)
    o_ref[...] = jnp.where(valid, x[:, 0:A_WINDOW] * LOG2E, -jnp.inf)


def _rel_table(w_rows, l):
    return pl.pallas_call(
        _rel_table_kernel,
        grid=(HEADS,),
        in_specs=[pl.BlockSpec((None, None, 1, 1024), lambda h: (l, h, 0, 0))],
        out_specs=pl.BlockSpec((None, ATT_T, A_WINDOW), lambda h: (h, 0, 0)),
        out_shape=jax.ShapeDtypeStruct((HEADS, ATT_T, A_WINDOW), f32),
        name="rel_table",
    )(w_rows)


def _pair_select(x, width):
    lane = lax.broadcasted_iota(jnp.int32, (1, width), 1)
    if width == PAIR_W:
        in_a = lane < HEAD_DIM
        in_b = lane >= HEAD_DIM
    else:
        in_a = (lane < 64) | ((lane >= 128) & (lane < 160))
        in_b = ((lane >= 64) & (lane < 128)) | ((lane >= 160) & (lane < 192))
    zero = jnp.zeros_like(x)
    return jnp.where(in_a, x, zero), jnp.where(in_b, x, zero)


def _attn_kernel(*refs, mode, dq, tq, pps):
    if mode == "mla":
        q_ref, k_ref, v_ref, o_ref = refs
        extra_ref = None
    else:
        q_ref, k_ref, v_ref, extra_ref, o_ref = refs
    i = pl.program_id(2)
    nq = SEQ // tq
    neg = jnp.float32(-jnp.inf)
    stack_pv = 2 * tq <= 512

    def attend(k0, nkeys, q0):
        lane = lax.broadcasted_iota(jnp.int32, (1, PAIR_W), 1)
        for p in range(pps):
            qa, qb = _pair_select(q_ref[:, p * dq:(p + 1) * dq], dq)
            kt = k_ref[pl.ds(k0, nkeys), p * dq:(p + 1) * dq]
            vt = v_ref[pl.ds(k0, nkeys), p * PAIR_W:(p + 1) * PAIR_W]
            s2 = _dot_t(jnp.concatenate([qa, qb], axis=0), kt)
            probs = []
            for head in range(2):
                s = s2[head * tq:(head + 1) * tq]
                if mode == "rel":
                    s = s + extra_ref[2 * p + head, :, A_WINDOW - nkeys:A_WINDOW]
                elif mode == "fox":
                    f = extra_ref[2 * p + head:2 * p + head + 1, :]
                    s = s + (f[:, q0:q0 + 1] - f[:, 0:nkeys]) * LOG2E
                if mode != "rel":
                    row = lax.broadcasted_iota(jnp.int32, (tq, tq), 0)
                    col = lax.broadcasted_iota(jnp.int32, (tq, tq), 1)
                    keep = (col <= row) if mode == "fox" else ((col >> 6) <= (row >> 6))
                    tail = jnp.where(keep, s[:, nkeys - tq:], neg)
                    s = tail if nkeys == tq else jnp.concatenate([s[:, :nkeys - tq], tail], axis=1)
                m = jnp.max(s, axis=-1, keepdims=True)
                probs.append(jnp.exp2(s - m).astype(bf16))
            v_ones = jnp.concatenate([vt, jnp.ones_like(vt)], axis=1)
            if stack_pv:
                o2 = _dot(jnp.concatenate(probs, axis=0), v_ones)
                oa, ob = o2[0:tq], o2[tq:]
            else:
                oa, ob = _dot(probs[0], v_ones), _dot(probs[1], v_ones)
            oa = oa[:, 0:PAIR_W] / oa[:, PAIR_W:]
            ob = ob[:, 0:PAIR_W] / ob[:, PAIR_W:]
            o_ref[:, p * PAIR_W:(p + 1) * PAIR_W] = jnp.where(lane < HEAD_DIM, oa, ob).astype(bf16)

    if mode == "rel":
        for ii in range(2):
            pl.when(i == ii)(functools.partial(attend, 0, (ii + 1) * tq, 0))

        @pl.when(i >= 2)
        def _():
            attend(pl.multiple_of((i - 2) * tq, tq), A_WINDOW, 0)
    else:
        for ii in range(nq):
            pl.when(i == ii)(functools.partial(attend, 0, (ii + 1) * tq, ii * tq))


def _attention(q, k, v, extra, *, mode, dq, tq, pps, kcol0, vcol0):
    nq = SEQ // tq
    groups = PAIRS // pps
    in_specs = [
        pl.BlockSpec((tq, pps * dq), lambda b, j, i: (b * nq + i, j)),
        pl.BlockSpec((SEQ, pps * dq), lambda b, j, i: (b, kcol0 // pps + j)),
        pl.BlockSpec((SEQ, pps * PAIR_W), lambda b, j, i: (b, vcol0 // pps + j)),
    ]
    args = [q, k, v]
    if mode == "fox":
        in_specs.append(pl.BlockSpec((None, 2 * pps, SEQ), lambda b, j, i: (j, 0, b)))
        args.append(extra.reshape(groups, 2 * pps, TOKENS))
    elif mode == "rel":
        assert tq == ATT_T
        in_specs.append(pl.BlockSpec((2 * pps, ATT_T, A_WINDOW), lambda b, j, i: (j, 0, 0)))
        args.append(extra)
    return pl.pallas_call(
        functools.partial(_attn_kernel, mode=mode, dq=dq, tq=tq, pps=pps),
        grid=(BATCH, groups, nq),
        in_specs=in_specs,
        out_specs=pl.BlockSpec((tq, pps * PAIR_W), lambda b, j, i: (b * nq + i, j)),
        out_shape=jax.ShapeDtypeStruct((TOKENS, BRANCH_W), bf16),
        compiler_params=_params("parallel", "parallel", "arbitrary"),
        name="attn_" + mode,
    )(*args)


def _merge_kernel(x_ref, ya_ref, yb_ref, yc_ref, gate_ref, wb_ref, wo_ref, g_ref, b_ref, o_ref):
    for r0 in range(0, x_ref.shape[0], SUB_ROWS):
        rows = slice(r0, r0 + SUB_ROWS)
        merged = None
        for n, y_ref in enumerate((ya_ref, yb_ref, yc_ref)):
            proj = _dot(y_ref[rows, :], wb_ref[n])
            term = gate_ref[rows, n * D_MODEL:(n + 1) * D_MODEL].astype(f32) * proj
            merged = term if merged is None else merged + term
        y = _dot(merged.astype(bf16), wo_ref[...])
        o_ref[rows, :] = _layer_norm(ALPHA * x_ref[rows, :] + y, g_ref[...], b_ref[...])


def _merge(x, ya, yb, yc, gates, wb, wo, g, b, l):
    tm = ROW_TILE
    row = lambda i: (i, 0)
    return pl.pallas_call(
        _merge_kernel,
        grid=(TOKENS // tm,),
        in_specs=[
            pl.BlockSpec((tm, D_MODEL), row),
            pl.BlockSpec((tm, BRANCH_W), row),
            pl.BlockSpec((tm, BRANCH_W), row),
            pl.BlockSpec((tm, BRANCH_W), row),
            pl.BlockSpec((tm, 3 * D_MODEL), row),
            _layer((3, BRANCH_W, D_MODEL), l),
            _layer((D_MODEL, D_MODEL), l),
            _layer((1, D_MODEL), l),
            _layer((1, D_MODEL), l),
        ],
        out_specs=pl.BlockSpec((tm, D_MODEL), row),
        out_shape=jax.ShapeDtypeStruct((TOKENS, D_MODEL), f32),
        compiler_params=_params("parallel"),
        name="merge",
    )(x, ya, yb, yc, gates, wb, wo, g, b)


def _xattn_kernel(x_ref, mem_ref, wq_ref, wkv_ref, wo_ref, g_ref, b_ref, o_ref, kv_ref):
    @pl.when(pl.program_id(1) == 0)
    def _():
        kv_ref[...] = _dot(mem_ref[...].astype(bf16), wkv_ref[...]).astype(bf16)

    scale = XA_HEAD_DIM ** -0.5 * LOG2E
    width = XA_HEADS * XA_HEAD_DIM
    for r0 in range(0, x_ref.shape[0], SUB_ROWS):
        rows = slice(r0, r0 + SUB_ROWS)
        x = x_ref[rows, :]
        q = (_dot(x.astype(bf16), wq_ref[...]) * scale).astype(bf16)
        outs = []
        for h in range(XA_HEADS):
            c0 = h * XA_HEAD_DIM
            s = _dot_t(q[:, c0:c0 + XA_HEAD_DIM], kv_ref[:, c0:c0 + XA_HEAD_DIM])
            p = jnp.exp2(s - jnp.max(s, axis=-1, keepdims=True))
            l = jnp.sum(p, axis=-1, keepdims=True)
            o = _dot(p.astype(bf16), kv_ref[:, width + c0:width + c0 + XA_HEAD_DIM])
            outs.append((o / l).astype(bf16))
        y = _dot(jnp.concatenate(outs, axis=-1), wo_ref[...])
        o_ref[rows, :] = _layer_norm(ALPHA * x + y, g_ref[...], b_ref[...])


def _xattn(x, mem, wq, wkv, wo, g, b, l):
    tm = ROW_TILE
    nq = SEQ // tm
    width = XA_HEADS * XA_HEAD_DIM
    return pl.pallas_call(
        _xattn_kernel,
        grid=(BATCH, nq),
        in_specs=[
            pl.BlockSpec((tm, D_MODEL), lambda bi, i: (bi * nq + i, 0)),
            pl.BlockSpec((MEM_LEN, D_MODEL), lambda bi, i: (bi, 0)),
            _layer((D_MODEL, width), l),
            _layer((D_MODEL, 2 * width), l),
            _layer((width, D_MODEL), l),
            _layer((1, D_MODEL), l),
            _layer((1, D_MODEL), l),
        ],
        out_specs=pl.BlockSpec((tm, D_MODEL), lambda bi, i: (bi * nq + i, 0)),
        out_shape=jax.ShapeDtypeStruct((TOKENS, D_MODEL), f32),
        scratch_shapes=[pltpu.VMEM((MEM_LEN, 2 * width), bf16)],
        compiler_params=_params("arbitrary", "arbitrary"),
        name="xattn",
    )(x, mem, wq, wkv, wo, g, b)


def _ffn_kernel(x_ref, wgu_ref, wd_ref, g_ref, b_ref, o_ref, acc_ref):
    for r0 in range(0, x_ref.shape[0], SUB_ROWS):
        rows = slice(r0, r0 + SUB_ROWS)
        x = x_ref[rows, :]
        xb = x.astype(bf16)
        for c in range(FFN_HIDDEN // FFN_CHUNK):
            c0 = c * FFN_CHUNK
            gate = _dot(xb, wgu_ref[:, c0:c0 + FFN_CHUNK])
            up = _dot(xb, wgu_ref[:, FFN_HIDDEN + c0:FFN_HIDDEN + c0 + FFN_CHUNK])
            h = (gate / (1.0 + jnp.exp(-gate)) * up).astype(bf16)
            part = _dot(h, wd_ref[c0:c0 + FFN_CHUNK, :])
            if c == 0:
                acc_ref[rows, :] = part
            else:
                acc_ref[rows, :] += part
        o_ref[rows, :] = _layer_norm(ALPHA * x + acc_ref[rows, :], g_ref[...], b_ref[...])


def _ffn(x, wgu, wd, g, b, l):
    tm = ROW_TILE
    row = lambda i: (i, 0)
    return pl.pallas_call(
        _ffn_kernel,
        grid=(TOKENS // tm,),
        in_specs=[
            pl.BlockSpec((tm, D_MODEL), row),
            _layer((D_MODEL, 2 * FFN_HIDDEN), l),
            _layer((FFN_HIDDEN, D_MODEL), l),
            _layer((1, D_MODEL), l),
            _layer((1, D_MODEL), l),
        ],
        out_specs=pl.BlockSpec((tm, D_MODEL), row),
        out_shape=jax.ShapeDtypeStruct((TOKENS, D_MODEL), f32),
        scratch_shapes=[pltpu.VMEM((tm, D_MODEL), f32)],
        compiler_params=_params("parallel"),
        name="ffn",
    )(x, wgu, wd, g, b)


def _arrange_w_in(w):
    a = w[:, :, 0:1536].astype(bf16)
    c = w[:, :, 2208:3744].astype(bf16)
    g = w[:, :, 3752:6824].astype(bf16)
    small = w[:, :, 1536:2208].astype(bf16)
    cq_ckv = small[:, :, 0:640]
    kr = small[:, :, 640:672]
    cf = w[:, :, 3744:3752].astype(bf16)
    half = B_ROPE // 2
    kr_sw = jnp.concatenate([kr[:, :, half:], kr[:, :, :half]], axis=2)
    lead = w.shape[:2]
    lat = jnp.concatenate([cq_ckv, kr, kr, cf, jnp.zeros(lead + (56,), bf16),
                           kr_sw, kr_sw, jnp.zeros(lead + (64,), bf16)], axis=2)
    return a, c, g, lat


def _arrange_w_uq(w):
    lead = w.shape[:2]
    w = w.reshape(lead + (PAIRS, 2, B_NOPE + B_ROPE))
    nope = w[..., :B_NOPE].reshape(lead + (PAIRS, 2 * B_NOPE))
    pe = w[..., B_NOPE:]
    half = B_ROPE // 2
    pe_sw = jnp.concatenate([pe[..., half:], pe[..., :half]], axis=-1)
    pe = pe.reshape(lead + (PAIRS, 2 * B_ROPE))
    pe_sw = pe_sw.reshape(lead + (PAIRS, 2 * B_ROPE))
    return jnp.concatenate([nope, pe, pe_sw], axis=-1).reshape(lead + (PAIRS * B_PAIR_W,)).astype(bf16)


def _arrange_w_ukv(w):
    lead = w.shape[:2]
    w = w.reshape(lead + (HEADS, B_NOPE + HEAD_DIM))
    wk = w[..., :B_NOPE].reshape(lead + (BRANCH_W,))
    wv = w[..., B_NOPE:].reshape(lead + (BRANCH_W,))
    return wk.astype(bf16), wv.astype(bf16)


def _rel_bias_rows(rel_bias):
    lead = rel_bias.shape[:2]
    lo = jnp.broadcast_to(rel_bias[..., :1], lead + (1024 - 512 - rel_bias.shape[-1],))
    hi = jnp.broadcast_to(rel_bias[..., -1:], lead + (512,))
    rows = jnp.concatenate([hi, rel_bias[..., ::-1], lo], axis=-1).astype(f32)
    return rows.reshape(lead + (1, 1024))


def kernel(x, mem, positions, ln_mix_g, ln_mix_b, w_in, b_gate, b_forget, a_rel_bias, b_q_norm, b_kv_norm, b_w_uq, b_w_ukv, w_branch, w_mix_out, ln_xa_g, ln_xa_b, xa_w_q, xa_w_kv, xa_w_o, ln_ffn_g, ln_ffn_b, ffn_w_gu, ffn_w_down):
    xf = x.reshape(TOKENS, D_MODEL)
    memf = mem.reshape(BATCH * MEM_LEN, D_MODEL)
    pos = positions.reshape(TOKENS, 1)

    half = B_ROPE // 2
    inv_freq = ROPE_BASE ** (-jnp.arange(half, dtype=f32) / half)
    freq_row = jnp.concatenate([jnp.tile(inv_freq, 4), jnp.zeros((64,), f32)]).reshape(1, 128)
    sign_row = jnp.concatenate([jnp.tile(jnp.concatenate([-jnp.ones((half,), f32), jnp.ones((half,), f32)]), 2),
                                jnp.zeros((64,), f32)]).reshape(1, 128)
    cos_t, sin_t = _rope_tables(pos, freq_row, sign_row)

    vec = lambda p: p.reshape(DEPTH, 1, p.shape[-1])
    wa, wc, wg, wl = _arrange_w_in(w_in)
    wq = _arrange_w_uq(b_w_uq)
    wk, wv = _arrange_w_ukv(b_w_ukv)
    rel_rows = _rel_bias_rows(a_rel_bias)
    w_branch_b, w_out_b = w_branch.astype(bf16), w_mix_out.astype(bf16)
    xa_q, xa_kv, xa_o = xa_w_q.astype(bf16), xa_w_kv.astype(bf16), xa_w_o.astype(bf16)
    w_gu, w_down = ffn_w_gu.astype(bf16), ffn_w_down.astype(bf16)
    bg = b_gate.reshape(DEPTH, 1, 3 * D_MODEL)
    bf = b_forget.reshape(DEPTH, HEADS, 1)

    for l in range(DEPTH):
        qkv_a, qkv_c, gates, cf_t, q_b, k_b, v_b = _inproj(
            xf, wa, wc, wg, wl, bg, cos_t, sin_t, vec(b_q_norm), vec(b_kv_norm), wq, wk, wv, l)
        forget = _forget_cumsum(cf_t, bf, l)
        rel_table = _rel_table(rel_rows, l)

        y_a = _attention(qkv_a, qkv_a, qkv_a, rel_table, mode="rel", dq=PAIR_W, tq=ATT_T, pps=4,
                         kcol0=PAIRS, vcol0=2 * PAIRS)
        y_b = _attention(q_b, k_b, v_b, None, mode="mla", dq=B_PAIR_W, tq=CAUSAL_T, pps=4, kcol0=0, vcol0=0)
        y_c = _attention(qkv_c, qkv_c, qkv_c, forget, mode="fox", dq=PAIR_W, tq=CAUSAL_T, pps=2,
                         kcol0=PAIRS, vcol0=2 * PAIRS)

        xf = _merge(xf, y_a, y_b, y_c, gates, w_branch_b, w_out_b, vec(ln_mix_g), vec(ln_mix_b), l)
        xf = _xattn(xf, memf, xa_q, xa_kv, xa_o, vec(ln_xa_g), vec(ln_xa_b), l)
        xf = _ffn(xf, w_gu, w_down, vec(ln_ffn_g), vec(ln_ffn_b), l)
    return xf.reshape(BATCH, SEQ, D_MODEL)
```

```python
import functools
import math

import jax
import jax.numpy as jnp
from jax import lax
from jax.experimental import pallas as pl
from jax.experimental.pallas import tpu as pltpu

D_MODEL = 1024
BATCH = 8
SEQ = 2048
DEPTH = 2
TOKENS = BATCH * SEQ
CHUNK = 64
MEM_LEN = 256

HEADS = 8
HEAD_DIM = 64
PAIRS = HEADS // 2
PAIR_W = 2 * HEAD_DIM
BRANCH_W = HEADS * HEAD_DIM

A_LEFT_CHUNKS = 8
ATT_T = 256
A_WINDOW = 3 * ATT_T
CAUSAL_T = 512

B_Q_LORA = 384
B_KV_LORA = 256
B_NOPE = 64
B_ROPE = 32
B_PAIR_W = 256
ROPE_BASE = 10000.0

XA_HEADS = 4
XA_HEAD_DIM = 128
FFN_HIDDEN = 2816
FFN_CHUNK = 256

LN_EPS = 1e-5
RMS_EPS = 1e-6
ALPHA = (2 * DEPTH) ** 0.25
LOG2E = math.log2(math.e)

LAT_W = 896
CF_ROW0 = 64

ROW_TILE = 1024
SUB_ROWS = 512

VMEM_LIMIT = 56 * 1024 * 1024

bf16 = jnp.bfloat16
f32 = jnp.float32


def _dot(a, b):
    return jnp.dot(a, b, preferred_element_type=f32)


def _dot_t(a, b):
    return lax.dot_general(a, b, (((1,), (1,)), ((), ())), preferred_element_type=f32)


def _layer_norm(z, g, b):
    mu = jnp.mean(z, axis=-1, keepdims=True)
    zc = z - mu
    var = jnp.mean(zc * zc, axis=-1, keepdims=True)
    return zc * lax.rsqrt(var + LN_EPS) * g + b


def _rms_norm(z, g):
    ms = jnp.mean(z * z, axis=-1, keepdims=True)
    return z * lax.rsqrt(ms + RMS_EPS) * g


def _params(*sem):
    return pltpu.CompilerParams(dimension_semantics=sem, vmem_limit_bytes=VMEM_LIMIT)


def _resident(shape):
    zeros = (0,) * len(shape)
    return pl.BlockSpec(shape, lambda *_: zeros, pipeline_mode=pl.Buffered(1))


def _layer(shape, l):
    zeros = (0,) * len(shape)
    return pl.BlockSpec((None,) + tuple(shape), lambda *_: (l,) + zeros, pipeline_mode=pl.Buffered(1))


def _inproj_kernel(x_ref, wa_ref, wc_ref, wg_ref, wl_ref, bg_ref, cos_ref, sin_ref, gq_ref, gkv_ref,
                   wq_ref, wk_ref, wv_ref,
                   qkva_ref, qkvc_ref, gate_ref, cft_ref, qb_ref, kb_ref, vb_ref):
    xb = x_ref[...].astype(bf16)
    qk_scale = HEAD_DIM ** -0.5 * LOG2E

    for dst, w_ref in ((qkva_ref, wa_ref), (qkvc_ref, wc_ref)):
        dst[:, 0:512] = (_dot_t(xb, w_ref[0:512, :]) * qk_scale).astype(bf16)
        dst[:, 512:1024] = _dot_t(xb, w_ref[512:1024, :]).astype(bf16)
        dst[:, 1024:1536] = _dot_t(xb, w_ref[1024:1536, :]).astype(bf16)
    for c in range(6):
        z = _dot_t(xb, wg_ref[c * 512:(c + 1) * 512, :]) + bg_ref[:, c * 512:(c + 1) * 512]
        gate_ref[:, c * 512:(c + 1) * 512] = (1.0 / (1.0 + jnp.exp(-z))).astype(bf16)

    kr_cf = _dot_t(xb, wl_ref[640:768, :])
    cft_ref[...] = kr_cf.T
    cos_t = cos_ref[...]
    sin_t = sin_ref[...]
    scale = (B_NOPE + B_ROPE) ** -0.5 * LOG2E
    cqn = _rms_norm(_dot_t(xb, wl_ref[0:384, :]), gq_ref[...]).astype(bf16)
    q1 = _dot(cqn, wq_ref[...])
    ckvn = _rms_norm(_dot_t(xb, wl_ref[384:640, :]), gkv_ref[...]).astype(bf16)
    kn = _dot(ckvn, wk_ref[...])
    vb_ref[...] = _dot(ckvn, wv_ref[...]).astype(bf16)
    kpe = (kr_cf * cos_t + _dot_t(xb, wl_ref[768:896, :]) * sin_t).astype(bf16)
    for p in range(PAIRS):
        c0 = p * B_PAIR_W
        qb_ref[:, c0:c0 + 128] = (q1[:, c0:c0 + 128] * scale).astype(bf16)
        blk = q1[:, c0 + 128:c0 + 256]
        pe = blk * cos_t + pltpu.roll(blk, 64, 1) * sin_t
        qb_ref[:, c0 + 128:c0 + 256] = (pe * scale).astype(bf16)
        kb_ref[:, c0:c0 + 128] = kn[:, p * 128:(p + 1) * 128].astype(bf16)
        kb_ref[:, c0 + 128:c0 + 256] = kpe


def _inproj(x, wa, wc, wg, wl, bg, cos_t, sin_t, gq, gkv, wq, wk, wv, l):
    tm = 512
    row = lambda i: (i, 0)
    return pl.pallas_call(
        _inproj_kernel,
        grid=(TOKENS // tm,),
        in_specs=[
            pl.BlockSpec((tm, D_MODEL), row),
            _layer((1536, D_MODEL), l),
            _layer((1536, D_MODEL), l),
            _layer((3 * D_MODEL, D_MODEL), l),
            _layer((LAT_W, D_MODEL), l),
            _layer((1, 3 * D_MODEL), l),
            pl.BlockSpec((tm, 128), row),
            pl.BlockSpec((tm, 128), row),
            _layer((1, B_Q_LORA), l),
            _layer((1, B_KV_LORA), l),
            _layer((B_Q_LORA, PAIRS * B_PAIR_W), l),
            _layer((B_KV_LORA, BRANCH_W), l),
            _layer((B_KV_LORA, BRANCH_W), l),
        ],
        out_specs=[
            pl.BlockSpec((tm, 1536), row),
            pl.BlockSpec((tm, 1536), row),
            pl.BlockSpec((tm, 3 * D_MODEL), row),
            pl.BlockSpec((128, tm), lambda i: (0, i)),
            pl.BlockSpec((tm, PAIRS * B_PAIR_W), row),
            pl.BlockSpec((tm, PAIRS * B_PAIR_W), row),
            pl.BlockSpec((tm, BRANCH_W), row),
        ],
        out_shape=[
            jax.ShapeDtypeStruct((TOKENS, 1536), bf16),
            jax.ShapeDtypeStruct((TOKENS, 1536), bf16),
            jax.ShapeDtypeStruct((TOKENS, 3 * D_MODEL), bf16),
            jax.ShapeDtypeStruct((128, TOKENS), f32),
            jax.ShapeDtypeStruct((TOKENS, PAIRS * B_PAIR_W), bf16),
            jax.ShapeDtypeStruct((TOKENS, PAIRS * B_PAIR_W), bf16),
            jax.ShapeDtypeStruct((TOKENS, BRANCH_W), bf16),
        ],
        compiler_params=_params("parallel"),
        name="inproj",
    )(x, wa, wc, wg, wl, bg, cos_t, sin_t, gq, gkv, wq, wk, wv)


def _rope_kernel(pos_ref, freq_ref, sign_ref, cos_ref, sin_ref):
    ang = pos_ref[...].astype(f32) * freq_ref[...]
    live = (freq_ref[...] > 0.0).astype(f32)
    cos_ref[...] = jnp.cos(ang) * live
    sin_ref[...] = jnp.sin(ang) * sign_ref[...]


def _rope_tables(pos, freq, sign):
    tm = 2048
    row = lambda i: (i, 0)
    return pl.pallas_call(
        _rope_kernel,
        grid=(TOKENS // tm,),
        in_specs=[pl.BlockSpec((tm, 1), row), _resident((1, 128)), _resident((1, 128))],
        out_specs=[pl.BlockSpec((tm, 128), row), pl.BlockSpec((tm, 128), row)],
        out_shape=[jax.ShapeDtypeStruct((TOKENS, 128), f32), jax.ShapeDtypeStruct((TOKENS, 128), f32)],
        compiler_params=_params("parallel"),
        name="rope_tables",
    )(pos, freq, sign)


def _forget_kernel(cf_ref, bf_ref, o_ref):
    rows = BATCH * HEADS
    lane = lax.broadcasted_iota(jnp.int32, (rows, 128), 1)
    bias = jnp.concatenate([bf_ref[...]] * BATCH, axis=0)
    carry = None
    for blk in range(SEQ // 128):
        z = jnp.concatenate([cf_ref[:, b * SEQ + blk * 128:b * SEQ + (blk + 1) * 128]
                             for b in range(BATCH)], axis=0) + bias
        acc = jnp.minimum(z, 0.0) - jnp.log(1.0 + jnp.exp(-jnp.abs(z)))
        d = 1
        while d < 128:
            acc = acc + jnp.where(lane >= d, pltpu.roll(acc, d, 1), 0.0)
            d *= 2
        if blk > 0:
            acc = acc + carry
        for b in range(BATCH):
            o_ref[:, b * SEQ + blk * 128:b * SEQ + (blk + 1) * 128] = acc[b * HEADS:(b + 1) * HEADS]
        carry = acc[:, 127:128]


def _forget_cumsum(cf_t, bf, l):
    return pl.pallas_call(
        _forget_kernel,
        grid=(1,),
        in_specs=[
            pl.BlockSpec((HEADS, TOKENS), lambda i: (CF_ROW0 // HEADS, 0)),
            _layer((HEADS, 1), l),
        ],
        out_specs=pl.BlockSpec((HEADS, TOKENS), lambda i: (0, 0)),
        out_shape=jax.ShapeDtypeStruct((HEADS, TOKENS), f32),
        compiler_params=_params("arbitrary"),
        name="forget_cumsum",
    )(cf_t, bf)


def _rel_table_kernel(w_ref, o_ref):
    x = jnp.broadcast_to(w_ref[...], (ATT_T, 1024))
    x = pltpu.roll(x, 1024 - ATT_T, 1, stride=1, stride_axis=0)
    r = lax.broadcasted_iota(jnp.int32, (ATT_T, A_WINDOW), 0)
    c = lax.broadcasted_iota(jnp.int32, (ATT_T, A_WINDOW), 1)
    dchunk = (c >> 6) - (r >> 6)
    valid = (dchunk >= 0) & (dchunk <= A_LEFT_CHUNKS)
    o_ref[...] = jnp.where(valid, x[:, 0:A_WINDOW] * LOG2E, -jnp.inf)


def _rel_table(w_rows, l):
    return pl.pallas_call(
        _rel_table_kernel,
        grid=(HEADS,),
        in_specs=[pl.BlockSpec((None, None, 1, 1024), lambda h: (l, h, 0, 0))],
        out_specs=pl.BlockSpec((None, ATT_T, A_WINDOW), lambda h: (h, 0, 0)),
        out_shape=jax.ShapeDtypeStruct((HEADS, ATT_T, A_WINDOW), f32),
        name="rel_table",
    )(w_rows)


def _pair_select(x, width):
    lane = lax.broadcasted_iota(jnp.int32, (1, width), 1)
    if width == PAIR_W:
        in_a = lane < HEAD_DIM
        in_b = lane >= HEAD_DIM
    else:
        in_a = (lane < 64) | ((lane >= 128) & (lane < 160))
        in_b = ((lane >= 64) & (lane < 128)) | ((lane >= 160) & (lane < 192))
    zero = jnp.zeros_like(x)
    return jnp.where(in_a, x, zero), jnp.where(in_b, x, zero)


def _attn_kernel(*refs, mode, dq, tq, pps):
    if mode == "mla":
        q_ref, k_ref, v_ref, o_ref = refs
        extra_ref = None
    else:
        q_ref, k_ref, v_ref, extra_ref, o_ref = refs
    i = pl.program_id(2)
    nq = SEQ // tq
    neg = jnp.float32(-jnp.inf)
    stack_pv = 2 * tq <= 512

    def attend(k0, nkeys, q0):
        lane = lax.broadcasted_iota(jnp.int32, (1, PAIR_W), 1)
        for p in range(pps):
            qa, qb = _pair_select(q_ref[:, p * dq:(p + 1) * dq], dq)
            kt = k_ref[pl.ds(k0, nkeys), p * dq:(p + 1) * dq]
            vt = v_ref[pl.ds(k0, nkeys), p * PAIR_W:(p + 1) * PAIR_W]
            s2 = _dot_t(jnp.concatenate([qa, qb], axis=0), kt)
            probs = []
            for head in range(2):
                s = s2[head * tq:(head + 1) * tq]
                if mode == "rel":
                    s = s + extra_ref[2 * p + head, :, A_WINDOW - nkeys:A_WINDOW]
                elif mode == "fox":
                    f = extra_ref[2 * p + head:2 * p + head + 1, :]
                    s = s + (f[:, q0:q0 + 1] - f[:, 0:nkeys]) * LOG2E
                if mode != "rel":
                    row = lax.broadcasted_iota(jnp.int32, (tq, tq), 0)
                    col = lax.broadcasted_iota(jnp.int32, (tq, tq), 1)
                    keep = (col <= row) if mode == "fox" else ((col >> 6) <= (row >> 6))
                    tail = jnp.where(keep, s[:, nkeys - tq:], neg)
                    s = tail if nkeys == tq else jnp.concatenate([s[:, :nkeys - tq], tail], axis=1)
                m = jnp.max(s, axis=-1, keepdims=True)
                probs.append(jnp.exp2(s - m).astype(bf16))
            v_ones = jnp.concatenate([vt, jnp.ones_like(vt)], axis=1)
            if stack_pv:
                o2 = _dot(jnp.concatenate(probs, axis=0), v_ones)
                oa, ob = o2[0:tq], o2[tq:]
            else:
                oa, ob = _dot(probs[0], v_ones), _dot(probs[1], v_ones)
            oa = oa[:, 0:PAIR_W] / oa[:, PAIR_W:]
            ob = ob[:, 0:PAIR_W] / ob[:, PAIR_W:]
            o_ref[:, p * PAIR_W:(p + 1) * PAIR_W] = jnp.where(lane < HEAD_DIM, oa, ob).astype(bf16)

    if mode == "rel":
        for ii in range(2):
            pl.when(i == ii)(functools.partial(attend, 0, (ii + 1) * tq, 0))

        @pl.when(i >= 2)
        def _():
            attend(pl.multiple_of((i - 2) * tq, tq), A_WINDOW, 0)
    else:
        for ii in range(nq):
            pl.when(i == ii)(functools.partial(attend, 0, (ii + 1) * tq, ii * tq))


def _attention(q, k, v, extra, *, mode, dq, tq, pps, kcol0, vcol0):
    nq = SEQ // tq
    groups = PAIRS // pps
    in_specs = [
        pl.BlockSpec((tq, pps * dq), lambda b, j, i: (b * nq + i, j)),
        pl.BlockSpec((SEQ, pps * dq), lambda b, j, i: (b, kcol0 // pps + j)),
        pl.BlockSpec((SEQ, pps * PAIR_W), lambda b, j, i: (b, vcol0 // pps + j)),
    ]
    args = [q, k, v]
    if mode == "fox":
        in_specs.append(pl.BlockSpec((None, 2 * pps, SEQ), lambda b, j, i: (j, 0, b)))
        args.append(extra.reshape(groups, 2 * pps, TOKENS))
    elif mode == "rel":
        assert tq == ATT_T
        in_specs.append(pl.BlockSpec((2 * pps, ATT_T, A_WINDOW), lambda b, j, i: (j, 0, 0)))
        args.append(extra)
    return pl.pallas_call(
        functools.partial(_attn_kernel, mode=mode, dq=dq, tq=tq, pps=pps),
        grid=(BATCH, groups, nq),
        in_specs=in_specs,
        out_specs=pl.BlockSpec((tq, pps * PAIR_W), lambda b, j, i: (b * nq + i, j)),
        out_shape=jax.ShapeDtypeStruct((TOKENS, BRANCH_W), bf16),
        compiler_params=_params("parallel", "parallel", "arbitrary"),
        name="attn_" + mode,
    )(*args)


def _merge_kernel(x_ref, ya_ref, yb_ref, yc_ref, gate_ref, wb_ref, wo_ref, g_ref, b_ref, o_ref):
    for r0 in range(0, x_ref.shape[0], SUB_ROWS):
        rows = slice(r0, r0 + SUB_ROWS)
        merged = None
        for n, y_ref in enumerate((ya_ref, yb_ref, yc_ref)):
            proj = _dot(y_ref[rows, :], wb_ref[n])
            term = gate_ref[rows, n * D_MODEL:(n + 1) * D_MODEL].astype(f32) * proj
            merged = term if merged is None else merged + term
        y = _dot(merged.astype(bf16), wo_ref[...])
        o_ref[rows, :] = _layer_norm(ALPHA * x_ref[rows, :] + y, g_ref[...], b_ref[...])


def _merge(x, ya, yb, yc, gates, wb, wo, g, b, l):
    tm = ROW_TILE
    row = lambda i: (i, 0)
    return pl.pallas_call(
        _merge_kernel,
        grid=(TOKENS // tm,),
        in_specs=[
            pl.BlockSpec((tm, D_MODEL), row),
            pl.BlockSpec((tm, BRANCH_W), row),
            pl.BlockSpec((tm, BRANCH_W), row),
            pl.BlockSpec((tm, BRANCH_W), row),
            pl.BlockSpec((tm, 3 * D_MODEL), row),
            _layer((3, BRANCH_W, D_MODEL), l),
            _layer((D_MODEL, D_MODEL), l),
            _layer((1, D_MODEL), l),
            _layer((1, D_MODEL), l),
        ],
        out_specs=pl.BlockSpec((tm, D_MODEL), row),
        out_shape=jax.ShapeDtypeStruct((TOKENS, D_MODEL), f32),
        compiler_params=_params("parallel"),
        name="merge",
    )(x, ya, yb, yc, gates, wb, wo, g, b)


def _xattn_kernel(x_ref, mem_ref, wq_ref, wkv_ref, wo_ref, g_ref, b_ref, o_ref, kv_ref):
    @pl.when(pl.program_id(1) == 0)
    def _():
        kv_ref[...] = _dot(mem_ref[...].astype(bf16), wkv_ref[...]).astype(bf16)

    scale = XA_HEAD_DIM ** -0.5 * LOG2E
    width = XA_HEADS * XA_HEAD_DIM
    for r0 in range(0, x_ref.shape[0], SUB_ROWS):
        rows = slice(r0, r0 + SUB_ROWS)
        x = x_ref[rows, :]
        q = (_dot(x.astype(bf16), wq_ref[...]) * scale).astype(bf16)
        outs = []
        for h in range(XA_HEADS):
            c0 = h * XA_HEAD_DIM
            s = _dot_t(q[:, c0:c0 + XA_HEAD_DIM], kv_ref[:, c0:c0 + XA_HEAD_DIM])
            p = jnp.exp2(s - jnp.max(s, axis=-1, keepdims=True))
            l = jnp.sum(p, axis=-1, keepdims=True)
            o = _dot(p.astype(bf16), kv_ref[:, width + c0:width + c0 + XA_HEAD_DIM])
            outs.append((o / l).astype(bf16))
        y = _dot(jnp.concatenate(outs, axis=-1), wo_ref[...])
        o_ref[rows, :] = _layer_norm(ALPHA * x + y, g_ref[...], b_ref[...])


def _xattn(x, mem, wq, wkv, wo, g, b, l):
    tm = ROW_TILE
    nq = SEQ // tm
    width = XA_HEADS * XA_HEAD_DIM
    return pl.pallas_call(
        _xattn_kernel,
        grid=(BATCH, nq),
        in_specs=[
            pl.BlockSpec((tm, D_MODEL), lambda bi, i: (bi * nq + i, 0)),
            pl.BlockSpec((MEM_LEN, D_MODEL), lambda bi, i: (bi, 0)),
            _layer((D_MODEL, width), l),
            _layer((D_MODEL, 2 * width), l),
            _layer((width, D_MODEL), l),
            _layer((1, D_MODEL), l),
            _layer((1, D_MODEL), l),
        ],
        out_specs=pl.BlockSpec((tm, D_MODEL), lambda bi, i: (bi * nq + i, 0)),
        out_shape=jax.ShapeDtypeStruct((TOKENS, D_MODEL), f32),
        scratch_shapes=[pltpu.VMEM((MEM_LEN, 2 * width), bf16)],
        compiler_params=_params("arbitrary", "arbitrary"),
        name="xattn",
    )(x, mem, wq, wkv, wo, g, b)


def _ffn_kernel(x_ref, wgu_ref, wd_ref, g_ref, b_ref, o_ref, acc_ref):
    for r0 in range(0, x_ref.shape[0], SUB_ROWS):
        rows = slice(r0, r0 + SUB_ROWS)
        x = x_ref[rows, :]
        xb = x.astype(bf16)
        for c in range(FFN_HIDDEN // FFN_CHUNK):
            c0 = c * FFN_CHUNK
            gate = _dot(xb, wgu_ref[:, c0:c0 + FFN_CHUNK])
            up = _dot(xb, wgu_ref[:, FFN_HIDDEN + c0:FFN_HIDDEN + c0 + FFN_CHUNK])
            h = (gate / (1.0 + jnp.exp(-gate)) * up).astype(bf16)
            part = _dot(h, wd_ref[c0:c0 + FFN_CHUNK, :])
            if c == 0:
                acc_ref[rows, :] = part
            else:
                acc_ref[rows, :] += part
        o_ref[rows, :] = _layer_norm(ALPHA * x + acc_ref[rows, :], g_ref[...], b_ref[...])


def _ffn(x, wgu, wd, g, b, l):
    tm = ROW_TILE
    row = lambda i: (i, 0)
    return pl.pallas_call(
        _ffn_kernel,
        grid=(TOKENS // tm,),
        in_specs=[
            pl.BlockSpec((tm, D_MODEL), row),
            _layer((D_MODEL, 2 * FFN_HIDDEN), l),
            _layer((FFN_HIDDEN, D_MODEL), l),
            _layer((1, D_MODEL), l),
            _layer((1, D_MODEL), l),
        ],
        out_specs=pl.BlockSpec((tm, D_MODEL), row),
        out_shape=jax.ShapeDtypeStruct((TOKENS, D_MODEL), f32),
        scratch_shapes=[pltpu.VMEM((tm, D_MODEL), f32)],
        compiler_params=_params("parallel"),
        name="ffn",
    )(x, wgu, wd, g, b)


def _arrange_w_in(w):
    wt = jnp.swapaxes(w, 1, 2)
    a = wt[:, 0:1536].astype(bf16)
    c = wt[:, 2208:3744].astype(bf16)
    g = wt[:, 3752:6824].astype(bf16)
    small = wt[:, 1536:2208].astype(bf16)
    cq_ckv = small[:, 0:640]
    kr = small[:, 640:672]
    cf = wt[:, 3744:3752].astype(bf16)
    half = B_ROPE // 2
    kr_sw = jnp.concatenate([kr[:, half:], kr[:, :half]], axis=1)
    zeros = lambda n: jnp.zeros((w.shape[0], n, w.shape[1]), bf16)
    lat = jnp.concatenate([cq_ckv, kr, kr, cf, zeros(56), kr_sw, kr_sw, zeros(64)], axis=1)
    return a, c, g, lat


def _arrange_w_uq(w):
    lead = w.shape[:2]
    w = w.reshape(lead + (PAIRS, 2, B_NOPE + B_ROPE))
    nope = w[..., :B_NOPE].reshape(lead + (PAIRS, 2 * B_NOPE))
    pe = w[..., B_NOPE:]
    half = B_ROPE // 2
    pe_sw = jnp.concatenate([pe[..., half:], pe[..., :half]], axis=-1)
    pe = pe.reshape(lead + (PAIRS, 2 * B_ROPE))
    pe_sw = pe_sw.reshape(lead + (PAIRS, 2 * B_ROPE))
    return jnp.concatenate([nope, pe, pe_sw], axis=-1).reshape(lead + (PAIRS * B_PAIR_W,)).astype(bf16)


def _arrange_w_ukv(w):
    lead = w.shape[:2]
    w = w.reshape(lead + (HEADS, B_NOPE + HEAD_DIM))
    wk = w[..., :B_NOPE].reshape(lead + (BRANCH_W,))
    wv = w[..., B_NOPE:].reshape(lead + (BRANCH_W,))
    return wk.astype(bf16), wv.astype(bf16)


def _rel_bias_rows(rel_bias):
    lead = rel_bias.shape[:2]
    lo = jnp.broadcast_to(rel_bias[..., :1], lead + (1024 - 512 - rel_bias.shape[-1],))
    hi = jnp.broadcast_to(rel_bias[..., -1:], lead + (512,))
    rows = jnp.concatenate([hi, rel_bias[..., ::-1], lo], axis=-1).astype(f32)
    return rows.reshape(lead + (1, 1024))


def kernel(x, mem, positions, ln_mix_g, ln_mix_b, w_in, b_gate, b_forget, a_rel_bias, b_q_norm, b_kv_norm, b_w_uq, b_w_ukv, w_branch, w_mix_out, ln_xa_g, ln_xa_b, xa_w_q, xa_w_kv, xa_w_o, ln_ffn_g, ln_ffn_b, ffn_w_gu, ffn_w_down):
    xf = x.reshape(TOKENS, D_MODEL)
    memf = mem.reshape(BATCH * MEM_LEN, D_MODEL)
    pos = positions.reshape(TOKENS, 1)

    half = B_ROPE // 2
    inv_freq = ROPE_BASE ** (-jnp.arange(half, dtype=f32) / half)
    freq_row = jnp.concatenate([jnp.tile(inv_freq, 4), jnp.zeros((64,), f32)]).reshape(1, 128)
    sign_row = jnp.concatenate([jnp.tile(jnp.concatenate([-jnp.ones((half,), f32), jnp.ones((half,), f32)]), 2),
                                jnp.zeros((64,), f32)]).reshape(1, 128)
    cos_t, sin_t = _rope_tables(pos, freq_row, sign_row)

    vec = lambda p: p.reshape(DEPTH, 1, p.shape[-1])
    wa, wc, wg, wl = _arrange_w_in(w_in)
    wq = _arrange_w_uq(b_w_uq)
    wk, wv = _arrange_w_ukv(b_w_ukv)
    rel_rows = _rel_bias_rows(a_rel_bias)
    w_branch_b, w_out_b = w_branch.astype(bf16), w_mix_out.astype(bf16)
    xa_q, xa_kv, xa_o = xa_w_q.astype(bf16), xa_w_kv.astype(bf16), xa_w_o.astype(bf16)
    w_gu, w_down = ffn_w_gu.astype(bf16), ffn_w_down.astype(bf16)
    bg = b_gate.reshape(DEPTH, 1, 3 * D_MODEL)
    bf = b_forget.reshape(DEPTH, HEADS, 1)

    for l in range(DEPTH):
        qkv_a, qkv_c, gates, cf_t, q_b, k_b, v_b = _inproj(
            xf, wa, wc, wg, wl, bg, cos_t, sin_t, vec(b_q_norm), vec(b_kv_norm), wq, wk, wv, l)
        forget = _forget_cumsum(cf_t, bf, l)
        rel_table = _rel_table(rel_rows, l)

        y_a = _attention(qkv_a, qkv_a, qkv_a, rel_table, mode="rel", dq=PAIR_W, tq=ATT_T, pps=4,
                         kcol0=PAIRS, vcol0=2 * PAIRS)
        y_b = _attention(q_b, k_b, v_b, None, mode="mla", dq=B_PAIR_W, tq=CAUSAL_T, pps=4, kcol0=0, vcol0=0)
        y_c = _attention(qkv_c, qkv_c, qkv_c, forget, mode="fox", dq=PAIR_W, tq=CAUSAL_T, pps=2,
                         kcol0=PAIRS, vcol0=2 * PAIRS)

        xf = _merge(xf, y_a, y_b, y_c, gates, w_branch_b, w_out_b, vec(ln_mix_g), vec(ln_mix_b), l)
        xf = _xattn(xf, memf, xa_q, xa_kv, xa_o, vec(ln_xa_g), vec(ln_xa_b), l)
        xf = _ffn(xf, w_gu, w_down, vec(ln_ffn_g), vec(ln_ffn_b), l)
    return xf.reshape(BATCH, SEQ, D_MODEL)
```

```python
import functools
import math

import jax
import jax.numpy as jnp
from jax import lax
from jax.experimental import pallas as pl
from jax.experimental.pallas import tpu as pltpu

D_MODEL = 1024
BATCH = 8
SEQ = 2048
DEPTH = 2
TOKENS = BATCH * SEQ
CHUNK = 64
MEM_LEN = 256

HEADS = 8
HEAD_DIM = 64
PAIRS = HEADS // 2
PAIR_W = 2 * HEAD_DIM
BRANCH_W = HEADS * HEAD_DIM

A_LEFT_CHUNKS = 8
ATT_T = 256
A_WINDOW = 3 * ATT_T
CAUSAL_T = 512

B_Q_LORA = 384
B_KV_LORA = 256
B_NOPE = 64
B_ROPE = 32
B_PAIR_W = 256
ROPE_BASE = 10000.0

XA_HEADS = 4
XA_HEAD_DIM = 128
FFN_HIDDEN = 2816
FFN_CHUNK = 256

LN_EPS = 1e-5
RMS_EPS = 1e-6
ALPHA = (2 * DEPTH) ** 0.25
LOG2E = math.log2(math.e)

LAT_W = 896
CF_ROW0 = 64

ROW_TILE = 1024
SUB_ROWS = 512

VMEM_LIMIT = 56 * 1024 * 1024

bf16 = jnp.bfloat16
f32 = jnp.float32


def _dot(a, b):
    return jnp.dot(a, b, preferred_element_type=f32)


def _dot_t(a, b):
    return lax.dot_general(a, b, (((1,), (1,)), ((), ())), preferred_element_type=f32)


def _layer_norm(z, g, b):
    mu = jnp.mean(z, axis=-1, keepdims=True)
    zc = z - mu
    var = jnp.mean(zc * zc, axis=-1, keepdims=True)
    return zc * lax.rsqrt(var + LN_EPS) * g + b


def _rms_norm(z, g):
    ms = jnp.mean(z * z, axis=-1, keepdims=True)
    return z * lax.rsqrt(ms + RMS_EPS) * g


def _params(*sem):
    return pltpu.CompilerParams(dimension_semantics=sem, vmem_limit_bytes=VMEM_LIMIT)


def _resident(shape):
    zeros = (0,) * len(shape)
    return pl.BlockSpec(shape, lambda *_: zeros, pipeline_mode=pl.Buffered(1))


def _layer(shape, l):
    zeros = (0,) * len(shape)
    return pl.BlockSpec((None,) + tuple(shape), lambda *_: (l,) + zeros, pipeline_mode=pl.Buffered(1))


def _inproj_kernel(x_ref, wa_ref, wc_ref, wl_ref, cos_ref, sin_ref, gq_ref, gkv_ref,
                   wq_ref, wk_ref, wv_ref,
                   qkva_ref, qkvc_ref, cft_ref, qb_ref, kb_ref, vb_ref):
    qk_scale = HEAD_DIM ** -0.5 * LOG2E
    scale = (B_NOPE + B_ROPE) ** -0.5 * LOG2E
    for r0 in range(0, x_ref.shape[0], SUB_ROWS):
        rows = slice(r0, r0 + SUB_ROWS)
        xb = x_ref[rows, :].astype(bf16)

        for dst, w_ref in ((qkva_ref, wa_ref), (qkvc_ref, wc_ref)):
            dst[rows, 0:512] = (_dot_t(xb, w_ref[0:512, :]) * qk_scale).astype(bf16)
            dst[rows, 512:1024] = _dot_t(xb, w_ref[512:1024, :]).astype(bf16)
            dst[rows, 1024:1536] = _dot_t(xb, w_ref[1024:1536, :]).astype(bf16)

        kr_cf = _dot_t(xb, wl_ref[640:768, :])
        cft_ref[:, rows] = kr_cf.T
        cos_t = cos_ref[rows, :]
        sin_t = sin_ref[rows, :]
        cqn = _rms_norm(_dot_t(xb, wl_ref[0:384, :]), gq_ref[...]).astype(bf16)
        q1 = _dot(cqn, wq_ref[...])
        ckvn = _rms_norm(_dot_t(xb, wl_ref[384:640, :]), gkv_ref[...]).astype(bf16)
        kn = _dot(ckvn, wk_ref[...])
        vb_ref[rows, :] = _dot(ckvn, wv_ref[...]).astype(bf16)
        kpe = (kr_cf * cos_t + _dot_t(xb, wl_ref[768:896, :]) * sin_t).astype(bf16)
        for p in range(PAIRS):
            c0 = p * B_PAIR_W
            qb_ref[rows, c0:c0 + 128] = (q1[:, c0:c0 + 128] * scale).astype(bf16)
            blk = q1[:, c0 + 128:c0 + 256]
            pe = blk * cos_t + pltpu.roll(blk, 64, 1) * sin_t
            qb_ref[rows, c0 + 128:c0 + 256] = (pe * scale).astype(bf16)
            kb_ref[rows, c0:c0 + 128] = kn[:, p * 128:(p + 1) * 128].astype(bf16)
            kb_ref[rows, c0 + 128:c0 + 256] = kpe


def _inproj(x, wa, wc, wl, cos_t, sin_t, gq, gkv, wq, wk, wv, l):
    tm = ROW_TILE
    row = lambda i: (i, 0)
    return pl.pallas_call(
        _inproj_kernel,
        grid=(TOKENS // tm,),
        in_specs=[
            pl.BlockSpec((tm, D_MODEL), row),
            _layer((1536, D_MODEL), l),
            _layer((1536, D_MODEL), l),
            _layer((LAT_W, D_MODEL), l),
            pl.BlockSpec((tm, 128), row),
            pl.BlockSpec((tm, 128), row),
            _layer((1, B_Q_LORA), l),
            _layer((1, B_KV_LORA), l),
            _layer((B_Q_LORA, PAIRS * B_PAIR_W), l),
            _layer((B_KV_LORA, BRANCH_W), l),
            _layer((B_KV_LORA, BRANCH_W), l),
        ],
        out_specs=[
            pl.BlockSpec((tm, 1536), row),
            pl.BlockSpec((tm, 1536), row),
            pl.BlockSpec((128, tm), lambda i: (0, i)),
            pl.BlockSpec((tm, PAIRS * B_PAIR_W), row),
            pl.BlockSpec((tm, PAIRS * B_PAIR_W), row),
            pl.BlockSpec((tm, BRANCH_W), row),
        ],
        out_shape=[
            jax.ShapeDtypeStruct((TOKENS, 1536), bf16),
            jax.ShapeDtypeStruct((TOKENS, 1536), bf16),
            jax.ShapeDtypeStruct((128, TOKENS), f32),
            jax.ShapeDtypeStruct((TOKENS, PAIRS * B_PAIR_W), bf16),
            jax.ShapeDtypeStruct((TOKENS, PAIRS * B_PAIR_W), bf16),
            jax.ShapeDtypeStruct((TOKENS, BRANCH_W), bf16),
        ],
        compiler_params=_params("parallel"),
        name="inproj",
    )(x, wa, wc, wl, cos_t, sin_t, gq, gkv, wq, wk, wv)


def _rope_kernel(pos_ref, freq_ref, sign_ref, cos_ref, sin_ref):
    ang = pos_ref[...].astype(f32) * freq_ref[...]
    live = (freq_ref[...] > 0.0).astype(f32)
    cos_ref[...] = jnp.cos(ang) * live
    sin_ref[...] = jnp.sin(ang) * sign_ref[...]


def _rope_tables(pos, freq, sign):
    tm = 2048
    row = lambda i: (i, 0)
    return pl.pallas_call(
        _rope_kernel,
        grid=(TOKENS // tm,),
        in_specs=[pl.BlockSpec((tm, 1), row), _resident((1, 128)), _resident((1, 128))],
        out_specs=[pl.BlockSpec((tm, 128), row), pl.BlockSpec((tm, 128), row)],
        out_shape=[jax.ShapeDtypeStruct((TOKENS, 128), f32), jax.ShapeDtypeStruct((TOKENS, 128), f32)],
        compiler_params=_params("parallel"),
        name="rope_tables",
    )(pos, freq, sign)


def _forget_kernel(cf_ref, bf_ref, o_ref):
    rows = BATCH * HEADS
    lane = lax.broadcasted_iota(jnp.int32, (rows, 128), 1)
    bias = jnp.concatenate([bf_ref[...]] * BATCH, axis=0)
    carry = None
    for blk in range(SEQ // 128):
        z = jnp.concatenate([cf_ref[:, b * SEQ + blk * 128:b * SEQ + (blk + 1) * 128]
                             for b in range(BATCH)], axis=0) + bias
        acc = jnp.minimum(z, 0.0) - jnp.log(1.0 + jnp.exp(-jnp.abs(z)))
        d = 1
        while d < 128:
            acc = acc + jnp.where(lane >= d, pltpu.roll(acc, d, 1), 0.0)
            d *= 2
        if blk > 0:
            acc = acc + carry
        for b in range(BATCH):
            o_ref[:, b * SEQ + blk * 128:b * SEQ + (blk + 1) * 128] = acc[b * HEADS:(b + 1) * HEADS]
        carry = acc[:, 127:128]


def _forget_cumsum(cf_t, bf, l):
    return pl.pallas_call(
        _forget_kernel,
        grid=(1,),
        in_specs=[
            pl.BlockSpec((HEADS, TOKENS), lambda i: (CF_ROW0 // HEADS, 0)),
            _layer((HEADS, 1), l),
        ],
        out_specs=pl.BlockSpec((HEADS, TOKENS), lambda i: (0, 0)),
        out_shape=jax.ShapeDtypeStruct((HEADS, TOKENS), f32),
        compiler_params=_params("arbitrary"),
        name="forget_cumsum",
    )(cf_t, bf)


def _rel_table_kernel(w_ref, o_ref):
    x = jnp.broadcast_to(w_ref[...], (ATT_T, 1024))
    x = pltpu.roll(x, 1024 - ATT_T, 1, stride=1, stride_axis=0)
    r = lax.broadcasted_iota(jnp.int32, (ATT_T, A_WINDOW), 0)
    c = lax.broadcasted_iota(jnp.int32, (ATT_T, A_WINDOW), 1)
    dchunk = (c >> 6) - (r >> 6)
    valid = (dchunk >= 0) & (dchunk <= A_LEFT_CHUNKS)
    o_ref[...] = jnp.where(valid, x[:, 0:A_WINDOW] * LOG2E, -jnp.inf)


def _rel_table(w_rows, l):
    return pl.pallas_call(
        _rel_table_kernel,
        grid=(HEADS,),
        in_specs=[pl.BlockSpec((None, None, 1, 1024), lambda h: (l, h, 0, 0))],
        out_specs=pl.BlockSpec((None, ATT_T, A_WINDOW), lambda h: (h, 0, 0)),
        out_shape=jax.ShapeDtypeStruct((HEADS, ATT_T, A_WINDOW), f32),
        name="rel_table",
    )(w_rows)


def _pair_select(x, width):
    lane = lax.broadcasted_iota(jnp.int32, (1, width), 1)
    if width == PAIR_W:
        in_a = lane < HEAD_DIM
        in_b = lane >= HEAD_DIM
    else:
        in_a = (lane < 64) | ((lane >= 128) & (lane < 160))
        in_b = ((lane >= 64) & (lane < 128)) | ((lane >= 160) & (lane < 192))
    zero = jnp.zeros_like(x)
    return jnp.where(in_a, x, zero), jnp.where(in_b, x, zero)


def _attn_kernel(*refs, mode, dq, tq, pps):
    if mode == "mla":
        q_ref, k_ref, v_ref, o_ref = refs
        extra_ref = None
    else:
        q_ref, k_ref, v_ref, extra_ref, o_ref = refs
    i = pl.program_id(2)
    nq = SEQ // tq
    neg = jnp.float32(-jnp.inf)
    stack_pv = 2 * tq <= 512

    def attend(k0, nkeys, q0):
        lane = lax.broadcasted_iota(jnp.int32, (1, PAIR_W), 1)
        for p in range(pps):
            qa, qb = _pair_select(q_ref[:, p * dq:(p + 1) * dq], dq)
            kt = k_ref[pl.ds(k0, nkeys), p * dq:(p + 1) * dq]
            vt = v_ref[pl.ds(k0, nkeys), p * PAIR_W:(p + 1) * PAIR_W]
            s2 = _dot_t(jnp.concatenate([qa, qb], axis=0), kt)
            probs = []
            for head in range(2):
                s = s2[head * tq:(head + 1) * tq]
                if mode == "rel":
                    s = s + extra_ref[2 * p + head, :, A_WINDOW - nkeys:A_WINDOW]
                elif mode == "fox":
                    f = extra_ref[2 * p + head:2 * p + head + 1, :]
                    s = s + (f[:, q0:q0 + 1] - f[:, 0:nkeys]) * LOG2E
                if mode != "rel":
                    row = lax.broadcasted_iota(jnp.int32, (tq, tq), 0)
                    col = lax.broadcasted_iota(jnp.int32, (tq, tq), 1)
                    keep = (col <= row) if mode == "fox" else ((col >> 6) <= (row >> 6))
                    tail = jnp.where(keep, s[:, nkeys - tq:], neg)
                    s = tail if nkeys == tq else jnp.concatenate([s[:, :nkeys - tq], tail], axis=1)
                m = jnp.max(s, axis=-1, keepdims=True)
                probs.append(jnp.exp2(s - m).astype(bf16))
            v_ones = jnp.concatenate([vt, jnp.ones_like(vt)], axis=1)
            if stack_pv:
                o2 = _dot(jnp.concatenate(probs, axis=0), v_ones)
                oa, ob = o2[0:tq], o2[tq:]
            else:
                oa, ob = _dot(probs[0], v_ones), _dot(probs[1], v_ones)
            oa = oa[:, 0:PAIR_W] / oa[:, PAIR_W:]
            ob = ob[:, 0:PAIR_W] / ob[:, PAIR_W:]
            o_ref[:, p * PAIR_W:(p + 1) * PAIR_W] = jnp.where(lane < HEAD_DIM, oa, ob).astype(bf16)

    if mode == "rel":
        for ii in range(2):
            pl.when(i == ii)(functools.partial(attend, 0, (ii + 1) * tq, 0))

        @pl.when(i >= 2)
        def _():
            attend(pl.multiple_of((i - 2) * tq, tq), A_WINDOW, 0)
    else:
        for ii in range(nq):
            pl.when(i == ii)(functools.partial(attend, 0, (ii + 1) * tq, ii * tq))


def _attention(q, k, v, extra, *, mode, dq, tq, pps, kcol0, vcol0):
    nq = SEQ // tq
    groups = PAIRS // pps
    in_specs = [
        pl.BlockSpec((tq, pps * dq), lambda b, j, i: (b * nq + i, j)),
        pl.BlockSpec((SEQ, pps * dq), lambda b, j, i: (b, kcol0 // pps + j)),
        pl.BlockSpec((SEQ, pps * PAIR_W), lambda b, j, i: (b, vcol0 // pps + j)),
    ]
    args = [q, k, v]
    if mode == "fox":
        in_specs.append(pl.BlockSpec((None, 2 * pps, SEQ), lambda b, j, i: (j, 0, b)))
        args.append(extra.reshape(groups, 2 * pps, TOKENS))
    elif mode == "rel":
        assert tq == ATT_T
        in_specs.append(pl.BlockSpec((2 * pps, ATT_T, A_WINDOW), lambda b, j, i: (j, 0, 0)))
        args.append(extra)
    return pl.pallas_call(
        functools.partial(_attn_kernel, mode=mode, dq=dq, tq=tq, pps=pps),
        grid=(BATCH, groups, nq),
        in_specs=in_specs,
        out_specs=pl.BlockSpec((tq, pps * PAIR_W), lambda b, j, i: (b * nq + i, j)),
        out_shape=jax.ShapeDtypeStruct((TOKENS, BRANCH_W), bf16),
        compiler_params=_params("parallel", "parallel", "arbitrary"),
        name="attn_" + mode,
    )(*args)


def _merge_kernel(x_ref, ya_ref, yb_ref, yc_ref, wg_ref, bg_ref, wb_ref, wo_ref, g_ref, b_ref, o_ref):
    for r0 in range(0, x_ref.shape[0], SUB_ROWS):
        rows = slice(r0, r0 + SUB_ROWS)
        x = x_ref[rows, :]
        xb = x.astype(bf16)
        merged = None
        for n, y_ref in enumerate((ya_ref, yb_ref, yc_ref)):
            cols = slice(n * D_MODEL, (n + 1) * D_MODEL)
            z = _dot_t(xb, wg_ref[cols, :]) + bg_ref[:, cols]
            term = _dot(y_ref[rows, :], wb_ref[n]) / (1.0 + jnp.exp(-z))
            merged = term if merged is None else merged + term
        y = _dot(merged.astype(bf16), wo_ref[...])
        o_ref[rows, :] = _layer_norm(ALPHA * x + y, g_ref[...], b_ref[...])


def _merge(x, ya, yb, yc, wg, bg, wb, wo, g, b, l):
    tm = ROW_TILE
    row = lambda i: (i, 0)
    return pl.pallas_call(
        _merge_kernel,
        grid=(TOKENS // tm,),
        in_specs=[
            pl.BlockSpec((tm, D_MODEL), row),
            pl.BlockSpec((tm, BRANCH_W), row),
            pl.BlockSpec((tm, BRANCH_W), row),
            pl.BlockSpec((tm, BRANCH_W), row),
            _layer((3 * D_MODEL, D_MODEL), l),
            _layer((1, 3 * D_MODEL), l),
            _layer((3, BRANCH_W, D_MODEL), l),
            _layer((D_MODEL, D_MODEL), l),
            _layer((1, D_MODEL), l),
            _layer((1, D_MODEL), l),
        ],
        out_specs=pl.BlockSpec((tm, D_MODEL), row),
        out_shape=jax.ShapeDtypeStruct((TOKENS, D_MODEL), f32),
        compiler_params=_params("parallel"),
        name="merge",
    )(x, ya, yb, yc, wg, bg, wb, wo, g, b)


def _xattn_kernel(x_ref, mem_ref, wq_ref, wkv_ref, wo_ref, g_ref, b_ref, o_ref, kv_ref):
    @pl.when(pl.program_id(1) == 0)
    def _():
        kv_ref[...] = _dot(mem_ref[...].astype(bf16), wkv_ref[...]).astype(bf16)

    scale = XA_HEAD_DIM ** -0.5 * LOG2E
    width = XA_HEADS * XA_HEAD_DIM
    for r0 in range(0, x_ref.shape[0], SUB_ROWS):
        rows = slice(r0, r0 + SUB_ROWS)
        x = x_ref[rows, :]
        q = (_dot(x.astype(bf16), wq_ref[...]) * scale).astype(bf16)
        outs = []
        for h in range(XA_HEADS):
            c0 = h * XA_HEAD_DIM
            s = _dot_t(q[:, c0:c0 + XA_HEAD_DIM], kv_ref[:, c0:c0 + XA_HEAD_DIM])
            p = jnp.exp2(s - jnp.max(s, axis=-1, keepdims=True))
            l = jnp.sum(p, axis=-1, keepdims=True)
            o = _dot(p.astype(bf16), kv_ref[:, width + c0:width + c0 + XA_HEAD_DIM])
            outs.append((o / l).astype(bf16))
        y = _dot(jnp.concatenate(outs, axis=-1), wo_ref[...])
        o_ref[rows, :] = _layer_norm(ALPHA * x + y, g_ref[...], b_ref[...])


def _xattn(x, mem, wq, wkv, wo, g, b, l):
    tm = ROW_TILE
    nq = SEQ // tm
    width = XA_HEADS * XA_HEAD_DIM
    return pl.pallas_call(
        _xattn_kernel,
        grid=(BATCH, nq),
        in_specs=[
            pl.BlockSpec((tm, D_MODEL), lambda bi, i: (bi * nq + i, 0)),
            pl.BlockSpec((MEM_LEN, D_MODEL), lambda bi, i: (bi, 0)),
            _layer((D_MODEL, width), l),
            _layer((D_MODEL, 2 * width), l),
            _layer((width, D_MODEL), l),
            _layer((1, D_MODEL), l),
            _layer((1, D_MODEL), l),
        ],
        out_specs=pl.BlockSpec((tm, D_MODEL), lambda bi, i: (bi * nq + i, 0)),
        out_shape=jax.ShapeDtypeStruct((TOKENS, D_MODEL), f32),
        scratch_shapes=[pltpu.VMEM((MEM_LEN, 2 * width), bf16)],
        compiler_params=_params("arbitrary", "arbitrary"),
        name="xattn",
    )(x, mem, wq, wkv, wo, g, b)


def _ffn_kernel(x_ref, wgu_ref, wd_ref, g_ref, b_ref, o_ref, acc_ref):
    for r0 in range(0, x_ref.shape[0], SUB_ROWS):
        rows = slice(r0, r0 + SUB_ROWS)
        x = x_ref[rows, :]
        xb = x.astype(bf16)
        for c in range(FFN_HIDDEN // FFN_CHUNK):
            c0 = c * FFN_CHUNK
            gate = _dot(xb, wgu_ref[:, c0:c0 + FFN_CHUNK])
            up = _dot(xb, wgu_ref[:, FFN_HIDDEN + c0:FFN_HIDDEN + c0 + FFN_CHUNK])
            h = (gate / (1.0 + jnp.exp(-gate)) * up).astype(bf16)
            part = _dot(h, wd_ref[c0:c0 + FFN_CHUNK, :])
            if c == 0:
                acc_ref[rows, :] = part
            else:
                acc_ref[rows, :] += part
        o_ref[rows, :] = _layer_norm(ALPHA * x + acc_ref[rows, :], g_ref[...], b_ref[...])


def _ffn(x, wgu, wd, g, b, l):
    tm = ROW_TILE
    row = lambda i: (i, 0)
    return pl.pallas_call(
        _ffn_kernel,
        grid=(TOKENS // tm,),
        in_specs=[
            pl.BlockSpec((tm, D_MODEL), row),
            _layer((D_MODEL, 2 * FFN_HIDDEN), l),
            _layer((FFN_HIDDEN, D_MODEL), l),
            _layer((1, D_MODEL), l),
            _layer((1, D_MODEL), l),
        ],
        out_specs=pl.BlockSpec((tm, D_MODEL), row),
        out_shape=jax.ShapeDtypeStruct((TOKENS, D_MODEL), f32),
        scratch_shapes=[pltpu.VMEM((tm, D_MODEL), f32)],
        compiler_params=_params("parallel"),
        name="ffn",
    )(x, wgu, wd, g, b)


def _arrange_w_in(w):
    wt = jnp.swapaxes(w, 1, 2)
    a = wt[:, 0:1536].astype(bf16)
    c = wt[:, 2208:3744].astype(bf16)
    g = wt[:, 3752:6824].astype(bf16)
    small = wt[:, 1536:2208].astype(bf16)
    cq_ckv = small[:, 0:640]
    kr = small[:, 640:672]
    cf = wt[:, 3744:3752].astype(bf16)
    half = B_ROPE // 2
    kr_sw = jnp.concatenate([kr[:, half:], kr[:, :half]], axis=1)
    zeros = lambda n: jnp.zeros((w.shape[0], n, w.shape[1]), bf16)
    lat = jnp.concatenate([cq_ckv, kr, kr, cf, zeros(56), kr_sw, kr_sw, zeros(64)], axis=1)
    return a, c, g, lat


def _arrange_w_uq(w):
    lead = w.shape[:2]
    w = w.reshape(lead + (PAIRS, 2, B_NOPE + B_ROPE))
    nope = w[..., :B_NOPE].reshape(lead + (PAIRS, 2 * B_NOPE))
    pe = w[..., B_NOPE:]
    half = B_ROPE // 2
    pe_sw = jnp.concatenate([pe[..., half:], pe[..., :half]], axis=-1)
    pe = pe.reshape(lead + (PAIRS, 2 * B_ROPE))
    pe_sw = pe_sw.reshape(lead + (PAIRS, 2 * B_ROPE))
    return jnp.concatenate([nope, pe, pe_sw], axis=-1).reshape(lead + (PAIRS * B_PAIR_W,)).astype(bf16)


def _arrange_w_ukv(w):
    lead = w.shape[:2]
    w = w.reshape(lead + (HEADS, B_NOPE + HEAD_DIM))
    wk = w[..., :B_NOPE].reshape(lead + (BRANCH_W,))
    wv = w[..., B_NOPE:].reshape(lead + (BRANCH_W,))
    return wk.astype(bf16), wv.astype(bf16)


def _rel_bias_rows(rel_bias):
    lead = rel_bias.shape[:2]
    lo = jnp.broadcast_to(rel_bias[..., :1], lead + (1024 - 512 - rel_bias.shape[-1],))
    hi = jnp.broadcast_to(rel_bias[..., -1:], lead + (512,))
    rows = jnp.concatenate([hi, rel_bias[..., ::-1], lo], axis=-1).astype(f32)
    return rows.reshape(lead + (1, 1024))


def kernel(x, mem, positions, ln_mix_g, ln_mix_b, w_in, b_gate, b_forget, a_rel_bias, b_q_norm, b_kv_norm, b_w_uq, b_w_ukv, w_branch, w_mix_out, ln_xa_g, ln_xa_b, xa_w_q, xa_w_kv, xa_w_o, ln_ffn_g, ln_ffn_b, ffn_w_gu, ffn_w_down):
    xf = x.reshape(TOKENS, D_MODEL)
    memf = mem.reshape(BATCH * MEM_LEN, D_MODEL)
    pos = positions.reshape(TOKENS, 1)

    half = B_ROPE // 2
    inv_freq = ROPE_BASE ** (-jnp.arange(half, dtype=f32) / half)
    freq_row = jnp.concatenate([jnp.tile(inv_freq, 4), jnp.zeros((64,), f32)]).reshape(1, 128)
    sign_row = jnp.concatenate([jnp.tile(jnp.concatenate([-jnp.ones((half,), f32), jnp.ones((half,), f32)]), 2),
                                jnp.zeros((64,), f32)]).reshape(1, 128)
    cos_t, sin_t = _rope_tables(pos, freq_row, sign_row)

    vec = lambda p: p.reshape(DEPTH, 1, p.shape[-1])
    wa, wc, wg, wl = _arrange_w_in(w_in)
    wq = _arrange_w_uq(b_w_uq)
    wk, wv = _arrange_w_ukv(b_w_ukv)
    rel_rows = _rel_bias_rows(a_rel_bias)
    w_branch_b, w_out_b = w_branch.astype(bf16), w_mix_out.astype(bf16)
    xa_q, xa_kv, xa_o = xa_w_q.astype(bf16), xa_w_kv.astype(bf16), xa_w_o.astype(bf16)
    w_gu, w_down = ffn_w_gu.astype(bf16), ffn_w_down.astype(bf16)
    bg = b_gate.reshape(DEPTH, 1, 3 * D_MODEL)
    bf = b_forget.reshape(DEPTH, HEADS, 1)

    for l in range(DEPTH):
        qkv_a, qkv_c, cf_t, q_b, k_b, v_b = _inproj(
            xf, wa, wc, wl, cos_t, sin_t, vec(b_q_norm), vec(b_kv_norm), wq, wk, wv, l)
        forget = _forget_cumsum(cf_t, bf, l)
        rel_table = _rel_table(rel_rows, l)

        y_a = _attention(qkv_a, qkv_a, qkv_a, rel_table, mode="rel", dq=PAIR_W, tq=ATT_T, pps=4,
                         kcol0=PAIRS, vcol0=2 * PAIRS)
        y_b = _attention(q_b, k_b, v_b, None, mode="mla", dq=B_PAIR_W, tq=CAUSAL_T, pps=4, kcol0=0, vcol0=0)
        y_c = _attention(qkv_c, qkv_c, qkv_c, forget, mode="fox", dq=PAIR_W, tq=CAUSAL_T, pps=2,
                         kcol0=PAIRS, vcol0=2 * PAIRS)

        xf = _merge(xf, y_a, y_b, y_c, wg, bg, w_branch_b, w_out_b, vec(ln_mix_g), vec(ln_mix_b), l)
        xf = _xattn(xf, memf, xa_q, xa_kv, xa_o, vec(ln_xa_g), vec(ln_xa_b), l)
        xf = _ffn(xf, w_gu, w_down, vec(ln_ffn_g), vec(ln_ffn_b), l)
    return xf.reshape(BATCH, SEQ, D_MODEL)
```

```python
import functools
import math

import jax
import jax.numpy as jnp
from jax import lax
from jax.experimental import pallas as pl
from jax.experimental.pallas import tpu as pltpu

D_MODEL = 1024
BATCH = 8
SEQ = 2048
DEPTH = 2
TOKENS = BATCH * SEQ
CHUNK = 64
MEM_LEN = 256

HEADS = 8
HEAD_DIM = 64
PAIRS = HEADS // 2
PAIR_W = 2 * HEAD_DIM
BRANCH_W = HEADS * HEAD_DIM

A_LEFT_CHUNKS = 8
ATT_T = 256
A_WINDOW = 3 * ATT_T
CAUSAL_T = 512

B_Q_LORA = 384
B_KV_LORA = 256
B_NOPE = 64
B_ROPE = 32
B_PAIR_W = 256
ROPE_BASE = 10000.0

XA_HEADS = 4
XA_HEAD_DIM = 128
FFN_HIDDEN = 2816
FFN_CHUNK = 256

LN_EPS = 1e-5
RMS_EPS = 1e-6
ALPHA = (2 * DEPTH) ** 0.25
LOG2E = math.log2(math.e)

QKV_W = 3 * BRANCH_W
LAT_CKV = B_Q_LORA
LAT_KR = B_Q_LORA + B_KV_LORA
LAT_W = LAT_KR + PAIR_W
CF_ROW0 = 2 * B_ROPE
REL_ROW = 1024

ROW_TILE = 1024
SUB_ROWS = 512

VMEM_LIMIT = 56 * 1024 * 1024

bf16 = jnp.bfloat16
f32 = jnp.float32


def _dot(a, b):
    return jnp.dot(a, b, preferred_element_type=f32)


def _dot_t(a, b):
    return lax.dot_general(a, b, (((1,), (1,)), ((), ())), preferred_element_type=f32)


def _layer_norm(z, g, b):
    mu = jnp.mean(z, axis=-1, keepdims=True)
    zc = z - mu
    var = jnp.mean(zc * zc, axis=-1, keepdims=True)
    return zc * lax.rsqrt(var + LN_EPS) * g + b


def _rms_norm(z, g):
    ms = jnp.mean(z * z, axis=-1, keepdims=True)
    return z * lax.rsqrt(ms + RMS_EPS) * g


def _params(*sem):
    return pltpu.CompilerParams(dimension_semantics=sem, vmem_limit_bytes=VMEM_LIMIT)


def _resident(shape):
    zeros = (0,) * len(shape)
    return pl.BlockSpec(shape, lambda *_: zeros, pipeline_mode=pl.Buffered(1))


def _layer(shape, l):
    zeros = (0,) * len(shape)
    return pl.BlockSpec((None,) + tuple(shape), lambda *_: (l,) + zeros, pipeline_mode=pl.Buffered(1))


def _inproj_kernel(x_ref, wa_ref, wc_ref, wl_ref, cos_ref, sin_ref, gq_ref, gkv_ref,
                   wq_ref, wk_ref, wv_ref,
                   qkva_ref, qkvc_ref, cft_ref, qb_ref, kb_ref, vb_ref):
    qk_scale = HEAD_DIM ** -0.5 * LOG2E
    scale = (B_NOPE + B_ROPE) ** -0.5 * LOG2E
    for r0 in range(0, x_ref.shape[0], SUB_ROWS):
        rows = slice(r0, r0 + SUB_ROWS)
        xb = x_ref[rows, :].astype(bf16)

        for dst, w_ref in ((qkva_ref, wa_ref), (qkvc_ref, wc_ref)):
            q, k, v = (slice(n * BRANCH_W, (n + 1) * BRANCH_W) for n in range(3))
            dst[rows, q] = (_dot_t(xb, w_ref[q, :]) * qk_scale).astype(bf16)
            dst[rows, k] = _dot_t(xb, w_ref[k, :]).astype(bf16)
            dst[rows, v] = _dot_t(xb, w_ref[v, :]).astype(bf16)

        kr_cf = _dot_t(xb, wl_ref[LAT_KR:LAT_W, :])
        cft_ref[:, rows] = kr_cf.T
        cos_t = cos_ref[rows, :]
        sin_t = sin_ref[rows, :]
        cqn = _rms_norm(_dot_t(xb, wl_ref[0:LAT_CKV, :]), gq_ref[...]).astype(bf16)
        q1 = _dot(cqn, wq_ref[...])
        ckvn = _rms_norm(_dot_t(xb, wl_ref[LAT_CKV:LAT_KR, :]), gkv_ref[...]).astype(bf16)
        kn = _dot(ckvn, wk_ref[...])
        vb_ref[rows, :] = _dot(ckvn, wv_ref[...]).astype(bf16)
        lane = lax.broadcasted_iota(jnp.int32, (1, PAIR_W), 1)
        kpe = jnp.where(lane < B_ROPE,
                        kr_cf * cos_t + pltpu.roll(kr_cf, PAIR_W - B_ROPE, 1) * sin_t,
                        pltpu.roll(kr_cf, B_ROPE, 1) * cos_t + kr_cf * sin_t).astype(bf16)
        for p in range(PAIRS):
            c0 = p * B_PAIR_W
            nope, rope_part = slice(c0, c0 + PAIR_W), slice(c0 + PAIR_W, c0 + B_PAIR_W)
            qb_ref[rows, nope] = (q1[:, nope] * scale).astype(bf16)
            blk = q1[:, rope_part]
            pe = blk * cos_t + pltpu.roll(blk, 2 * B_ROPE, 1) * sin_t
            qb_ref[rows, rope_part] = (pe * scale).astype(bf16)
            kb_ref[rows, nope] = kn[:, p * PAIR_W:(p + 1) * PAIR_W].astype(bf16)
            kb_ref[rows, rope_part] = kpe


def _inproj(x, wa, wc, wl, cos_t, sin_t, gq, gkv, wq, wk, wv, l):
    tm = ROW_TILE
    row = lambda i: (i, 0)
    return pl.pallas_call(
        _inproj_kernel,
        grid=(TOKENS // tm,),
        in_specs=[
            pl.BlockSpec((tm, D_MODEL), row),
            _layer((QKV_W, D_MODEL), l),
            _layer((QKV_W, D_MODEL), l),
            _layer((LAT_W, D_MODEL), l),
            pl.BlockSpec((tm, PAIR_W), row),
            pl.BlockSpec((tm, PAIR_W), row),
            _layer((1, B_Q_LORA), l),
            _layer((1, B_KV_LORA), l),
            _layer((B_Q_LORA, PAIRS * B_PAIR_W), l),
            _layer((B_KV_LORA, BRANCH_W), l),
            _layer((B_KV_LORA, BRANCH_W), l),
        ],
        out_specs=[
            pl.BlockSpec((tm, QKV_W), row),
            pl.BlockSpec((tm, QKV_W), row),
            pl.BlockSpec((PAIR_W, tm), lambda i: (0, i)),
            pl.BlockSpec((tm, PAIRS * B_PAIR_W), row),
            pl.BlockSpec((tm, PAIRS * B_PAIR_W), row),
            pl.BlockSpec((tm, BRANCH_W), row),
        ],
        out_shape=[
            jax.ShapeDtypeStruct((TOKENS, QKV_W), bf16),
            jax.ShapeDtypeStruct((TOKENS, QKV_W), bf16),
            jax.ShapeDtypeStruct((PAIR_W, TOKENS), f32),
            jax.ShapeDtypeStruct((TOKENS, PAIRS * B_PAIR_W), bf16),
            jax.ShapeDtypeStruct((TOKENS, PAIRS * B_PAIR_W), bf16),
            jax.ShapeDtypeStruct((TOKENS, BRANCH_W), bf16),
        ],
        compiler_params=_params("parallel"),
        name="inproj",
    )(x, wa, wc, wl, cos_t, sin_t, gq, gkv, wq, wk, wv)


def _rope_kernel(pos_ref, freq_ref, sign_ref, cos_ref, sin_ref):
    ang = pos_ref[...].astype(f32) * freq_ref[...]
    live = (freq_ref[...] > 0.0).astype(f32)
    cos_ref[...] = jnp.cos(ang) * live
    sin_ref[...] = jnp.sin(ang) * sign_ref[...]


def _rope_tables(pos, freq, sign):
    tm = 2048
    row = lambda i: (i, 0)
    return pl.pallas_call(
        _rope_kernel,
        grid=(TOKENS // tm,),
        in_specs=[pl.BlockSpec((tm, 1), row), _resident((1, PAIR_W)), _resident((1, PAIR_W))],
        out_specs=[pl.BlockSpec((tm, PAIR_W), row), pl.BlockSpec((tm, PAIR_W), row)],
        out_shape=[jax.ShapeDtypeStruct((TOKENS, PAIR_W), f32), jax.ShapeDtypeStruct((TOKENS, PAIR_W), f32)],
        compiler_params=_params("parallel"),
        name="rope_tables",
    )(pos, freq, sign)


def _forget_kernel(cf_ref, bf_ref, o_ref):
    rows = BATCH * HEADS
    lane = lax.broadcasted_iota(jnp.int32, (rows, 128), 1)
    bias = jnp.concatenate([bf_ref[...]] * BATCH, axis=0)
    carry = None
    for blk in range(SEQ // 128):
        z = jnp.concatenate([cf_ref[:, b * SEQ + blk * 128:b * SEQ + (blk + 1) * 128]
                             for b in range(BATCH)], axis=0) + bias
        acc = jnp.minimum(z, 0.0) - jnp.log(1.0 + jnp.exp(-jnp.abs(z)))
        d = 1
        while d < 128:
            acc = acc + jnp.where(lane >= d, pltpu.roll(acc, d, 1), 0.0)
            d *= 2
        if blk > 0:
            acc = acc + carry
        for b in range(BATCH):
            o_ref[:, b * SEQ + blk * 128:b * SEQ + (blk + 1) * 128] = acc[b * HEADS:(b + 1) * HEADS]
        carry = acc[:, 127:128]


def _forget_cumsum(cf_t, bf, l):
    return pl.pallas_call(
        _forget_kernel,
        grid=(1,),
        in_specs=[
            pl.BlockSpec((HEADS, TOKENS), lambda i: (CF_ROW0 // HEADS, 0)),
            _layer((HEADS, 1), l),
        ],
        out_specs=pl.BlockSpec((HEADS, TOKENS), lambda i: (0, 0)),
        out_shape=jax.ShapeDtypeStruct((HEADS, TOKENS), f32),
        compiler_params=_params("arbitrary"),
        name="forget_cumsum",
    )(cf_t, bf)


def _rel_table_kernel(w_ref, o_ref):
    x = jnp.broadcast_to(w_ref[...], (ATT_T, REL_ROW))
    x = pltpu.roll(x, REL_ROW - ATT_T, 1, stride=1, stride_axis=0)
    r = lax.broadcasted_iota(jnp.int32, (ATT_T, A_WINDOW), 0)
    c = lax.broadcasted_iota(jnp.int32, (ATT_T, A_WINDOW), 1)
    dchunk = (c >> 6) - (r >> 6)
    valid = (dchunk >= 0) & (dchunk <= A_LEFT_CHUNKS)
    o_ref[...] = jnp.where(valid, x[:, 0:A_WINDOW] * LOG2E, -jnp.inf)


def _rel_table(w_rows, l):
    return pl.pallas_call(
        _rel_table_kernel,
        grid=(HEADS,),
        in_specs=[pl.BlockSpec((None, None, 1, REL_ROW), lambda h: (l, h, 0, 0))],
        out_specs=pl.BlockSpec((None, ATT_T, A_WINDOW), lambda h: (h, 0, 0)),
        out_shape=jax.ShapeDtypeStruct((HEADS, ATT_T, A_WINDOW), f32),
        name="rel_table",
    )(w_rows)


def _pair_select(x, width):
    lane = lax.broadcasted_iota(jnp.int32, (1, width), 1)
    if width == PAIR_W:
        in_a = lane < HEAD_DIM
        in_b = lane >= HEAD_DIM
    else:
        in_a = (lane < 64) | ((lane >= 128) & (lane < 160))
        in_b = ((lane >= 64) & (lane < 128)) | ((lane >= 160) & (lane < 192))
    zero = jnp.zeros_like(x)
    return jnp.where(in_a, x, zero), jnp.where(in_b, x, zero)


def _attn_kernel(*refs, mode, dq, tq, pps):
    if mode == "mla":
        q_ref, k_ref, v_ref, o_ref = refs
        extra_ref = None
    else:
        q_ref, k_ref, v_ref, extra_ref, o_ref = refs
    i = pl.program_id(2)
    nq = SEQ // tq
    neg = jnp.float32(-jnp.inf)
    stack_pv = 2 * tq <= 512

    def attend(k0, nkeys, q0):
        lane = lax.broadcasted_iota(jnp.int32, (1, PAIR_W), 1)
        for p in range(pps):
            qa, qb = _pair_select(q_ref[:, p * dq:(p + 1) * dq], dq)
            kt = k_ref[pl.ds(k0, nkeys), p * dq:(p + 1) * dq]
            vt = v_ref[pl.ds(k0, nkeys), p * PAIR_W:(p + 1) * PAIR_W]
            s2 = _dot_t(jnp.concatenate([qa, qb], axis=0), kt)
            probs = []
            for head in range(2):
                s = s2[head * tq:(head + 1) * tq]
                if mode == "rel":
                    s = s + extra_ref[2 * p + head, :, A_WINDOW - nkeys:A_WINDOW]
                elif mode == "fox":
                    f = extra_ref[2 * p + head:2 * p + head + 1, :]
                    s = s + (f[:, q0:q0 + 1] - f[:, 0:nkeys]) * LOG2E
                if mode != "rel":
                    row = lax.broadcasted_iota(jnp.int32, (tq, tq), 0)
                    col = lax.broadcasted_iota(jnp.int32, (tq, tq), 1)
                    keep = (col <= row) if mode == "fox" else ((col >> 6) <= (row >> 6))
                    tail = jnp.where(keep, s[:, nkeys - tq:], neg)
                    s = tail if nkeys == tq else jnp.concatenate([s[:, :nkeys - tq], tail], axis=1)
                m = jnp.max(s, axis=-1, keepdims=True)
                probs.append(jnp.exp2(s - m).astype(bf16))
            v_ones = jnp.concatenate([vt, jnp.ones_like(vt)], axis=1)
            if stack_pv:
                o2 = _dot(jnp.concatenate(probs, axis=0), v_ones)
                oa, ob = o2[0:tq], o2[tq:]
            else:
                oa, ob = _dot(probs[0], v_ones), _dot(probs[1], v_ones)
            oa = oa[:, 0:PAIR_W] / oa[:, PAIR_W:]
            ob = ob[:, 0:PAIR_W] / ob[:, PAIR_W:]
            o_ref[:, p * PAIR_W:(p + 1) * PAIR_W] = jnp.where(lane < HEAD_DIM, oa, ob).astype(bf16)

    if mode == "rel":
        for ii in range(2):
            pl.when(i == ii)(functools.partial(attend, 0, (ii + 1) * tq, 0))

        @pl.when(i >= 2)
        def _():
            attend(pl.multiple_of((i - 2) * tq, tq), A_WINDOW, 0)
    else:
        for ii in range(nq):
            pl.when(i == ii)(functools.partial(attend, 0, (ii + 1) * tq, ii * tq))


def _attention(q, k, v, extra, *, mode, dq, tq, pps, kcol0, vcol0):
    nq = SEQ // tq
    groups = PAIRS // pps
    in_specs = [
        pl.BlockSpec((tq, pps * dq), lambda b, j, i: (b * nq + i, j)),
        pl.BlockSpec((SEQ, pps * dq), lambda b, j, i: (b, kcol0 // pps + j)),
        pl.BlockSpec((SEQ, pps * PAIR_W), lambda b, j, i: (b, vcol0 // pps + j)),
    ]
    args = [q, k, v]
    if mode == "fox":
        in_specs.append(pl.BlockSpec((None, 2 * pps, SEQ), lambda b, j, i: (j, 0, b)))
        args.append(extra.reshape(groups, 2 * pps, TOKENS))
    elif mode == "rel":
        assert tq == ATT_T
        in_specs.append(pl.BlockSpec((2 * pps, ATT_T, A_WINDOW), lambda b, j, i: (j, 0, 0)))
        args.append(extra)
    return pl.pallas_call(
        functools.partial(_attn_kernel, mode=mode, dq=dq, tq=tq, pps=pps),
        grid=(BATCH, groups, nq),
        in_specs=in_specs,
        out_specs=pl.BlockSpec((tq, pps * PAIR_W), lambda b, j, i: (b * nq + i, j)),
        out_shape=jax.ShapeDtypeStruct((TOKENS, BRANCH_W), bf16),
        compiler_params=_params("parallel", "parallel", "arbitrary"),
        name="attn_" + mode,
    )(*args)


def _merge_kernel(x_ref, ya_ref, yb_ref, yc_ref, wg_ref, bg_ref, wb_ref, wo_ref, g_ref, b_ref, o_ref):
    for r0 in range(0, x_ref.shape[0], SUB_ROWS):
        rows = slice(r0, r0 + SUB_ROWS)
        x = x_ref[rows, :]
        xb = x.astype(bf16)
        merged = None
        for n, y_ref in enumerate((ya_ref, yb_ref, yc_ref)):
            cols = slice(n * D_MODEL, (n + 1) * D_MODEL)
            z = _dot_t(xb, wg_ref[cols, :]) + bg_ref[:, cols]
            term = _dot(y_ref[rows, :], wb_ref[n]) / (1.0 + jnp.exp(-z))
            merged = term if merged is None else merged + term
        y = _dot(merged.astype(bf16), wo_ref[...])
        o_ref[rows, :] = _layer_norm(ALPHA * x + y, g_ref[...], b_ref[...])


def _merge(x, ya, yb, yc, wg, bg, wb, wo, g, b, l):
    tm = ROW_TILE
    row = lambda i: (i, 0)
    return pl.pallas_call(
        _merge_kernel,
        grid=(TOKENS // tm,),
        in_specs=[
            pl.BlockSpec((tm, D_MODEL), row),
            pl.BlockSpec((tm, BRANCH_W), row),
            pl.BlockSpec((tm, BRANCH_W), row),
            pl.BlockSpec((tm, BRANCH_W), row),
            _layer((3 * D_MODEL, D_MODEL), l),
            _layer((1, 3 * D_MODEL), l),
            _layer((3, BRANCH_W, D_MODEL), l),
            _layer((D_MODEL, D_MODEL), l),
            _layer((1, D_MODEL), l),
            _layer((1, D_MODEL), l),
        ],
        out_specs=pl.BlockSpec((tm, D_MODEL), row),
        out_shape=jax.ShapeDtypeStruct((TOKENS, D_MODEL), f32),
        compiler_params=_params("parallel"),
        name="merge",
    )(x, ya, yb, yc, wg, bg, wb, wo, g, b)


def _xattn_kernel(x_ref, mem_ref, wq_ref, wkv_ref, wo_ref, g_ref, b_ref, o_ref, kv_ref):
    @pl.when(pl.program_id(1) == 0)
    def _():
        kv_ref[...] = _dot(mem_ref[...].astype(bf16), wkv_ref[...]).astype(bf16)

    scale = XA_HEAD_DIM ** -0.5 * LOG2E
    width = XA_HEADS * XA_HEAD_DIM
    for r0 in range(0, x_ref.shape[0], SUB_ROWS):
        rows = slice(r0, r0 + SUB_ROWS)
        x = x_ref[rows, :]
        q = (_dot(x.astype(bf16), wq_ref[...]) * scale).astype(bf16)
        outs = []
        for h in range(XA_HEADS):
            c0 = h * XA_HEAD_DIM
            s = _dot_t(q[:, c0:c0 + XA_HEAD_DIM], kv_ref[:, c0:c0 + XA_HEAD_DIM])
            p = jnp.exp2(s - jnp.max(s, axis=-1, keepdims=True))
            l = jnp.sum(p, axis=-1, keepdims=True)
            o = _dot(p.astype(bf16), kv_ref[:, width + c0:width + c0 + XA_HEAD_DIM])
            outs.append((o / l).astype(bf16))
        y = _dot(jnp.concatenate(outs, axis=-1), wo_ref[...])
        o_ref[rows, :] = _layer_norm(ALPHA * x + y, g_ref[...], b_ref[...])


def _xattn(x, mem, wq, wkv, wo, g, b, l):
    tm = ROW_TILE
    nq = SEQ // tm
    width = XA_HEADS * XA_HEAD_DIM
    return pl.pallas_call(
        _xattn_kernel,
        grid=(BATCH, nq),
        in_specs=[
            pl.BlockSpec((tm, D_MODEL), lambda bi, i: (bi * nq + i, 0)),
            pl.BlockSpec((MEM_LEN, D_MODEL), lambda bi, i: (bi, 0)),
            _layer((D_MODEL, width), l),
            _layer((D_MODEL, 2 * width), l),
            _layer((width, D_MODEL), l),
            _layer((1, D_MODEL), l),
            _layer((1, D_MODEL), l),
        ],
        out_specs=pl.BlockSpec((tm, D_MODEL), lambda bi, i: (bi * nq + i, 0)),
        out_shape=jax.ShapeDtypeStruct((TOKENS, D_MODEL), f32),
        scratch_shapes=[pltpu.VMEM((MEM_LEN, 2 * width), bf16)],
        compiler_params=_params("arbitrary", "arbitrary"),
        name="xattn",
    )(x, mem, wq, wkv, wo, g, b)


def _ffn_kernel(x_ref, wgu_ref, wd_ref, g_ref, b_ref, o_ref, acc_ref):
    for r0 in range(0, x_ref.shape[0], SUB_ROWS):
        rows = slice(r0, r0 + SUB_ROWS)
        x = x_ref[rows, :]
        xb = x.astype(bf16)
        for c in range(FFN_HIDDEN // FFN_CHUNK):
            c0 = c * FFN_CHUNK
            gate = _dot(xb, wgu_ref[:, c0:c0 + FFN_CHUNK])
            up = _dot(xb, wgu_ref[:, FFN_HIDDEN + c0:FFN_HIDDEN + c0 + FFN_CHUNK])
            h = (gate / (1.0 + jnp.exp(-gate)) * up).astype(bf16)
            part = _dot(h, wd_ref[c0:c0 + FFN_CHUNK, :])
            if c == 0:
                acc_ref[rows, :] = part
            else:
                acc_ref[rows, :] += part
        o_ref[rows, :] = _layer_norm(ALPHA * x + acc_ref[rows, :], g_ref[...], b_ref[...])


def _ffn(x, wgu, wd, g, b, l):
    tm = ROW_TILE
    row = lambda i: (i, 0)
    return pl.pallas_call(
        _ffn_kernel,
        grid=(TOKENS // tm,),
        in_specs=[
            pl.BlockSpec((tm, D_MODEL), row),
            _layer((D_MODEL, 2 * FFN_HIDDEN), l),
            _layer((FFN_HIDDEN, D_MODEL), l),
            _layer((1, D_MODEL), l),
            _layer((1, D_MODEL), l),
        ],
        out_specs=pl.BlockSpec((tm, D_MODEL), row),
        out_shape=jax.ShapeDtypeStruct((TOKENS, D_MODEL), f32),
        scratch_shapes=[pltpu.VMEM((tm, D_MODEL), f32)],
        compiler_params=_params("parallel"),
        name="ffn",
    )(x, wgu, wd, g, b)


def _arrange_w_in(w):
    a0, b0 = 0, QKV_W
    c0 = b0 + B_Q_LORA + B_KV_LORA + B_ROPE
    f0 = c0 + QKV_W
    g0 = f0 + HEADS
    wt = jnp.swapaxes(w, 1, 2)
    a = wt[:, a0:b0].astype(bf16)
    c = wt[:, c0:f0].astype(bf16)
    g = wt[:, g0:g0 + 3 * D_MODEL].astype(bf16)
    small = wt[:, b0:c0].astype(bf16)
    cq_ckv = small[:, 0:LAT_KR]
    kr = small[:, LAT_KR:]
    cf = wt[:, f0:g0].astype(bf16)
    half = B_ROPE // 2
    kr_sw = jnp.concatenate([kr[:, half:], kr[:, :half]], axis=1)
    pad = jnp.zeros((w.shape[0], PAIR_W - 2 * B_ROPE - HEADS, w.shape[1]), bf16)
    lat = jnp.concatenate([cq_ckv, kr, kr_sw, cf, pad], axis=1)
    return a, c, g, lat


def _arrange_w_uq(w):
    lead = w.shape[:2]
    w = w.reshape(lead + (PAIRS, 2, B_NOPE + B_ROPE))
    nope = w[..., :B_NOPE].reshape(lead + (PAIRS, 2 * B_NOPE))
    pe = w[..., B_NOPE:]
    half = B_ROPE // 2
    pe_sw = jnp.concatenate([pe[..., half:], pe[..., :half]], axis=-1)
    pe = pe.reshape(lead + (PAIRS, 2 * B_ROPE))
    pe_sw = pe_sw.reshape(lead + (PAIRS, 2 * B_ROPE))
    return jnp.concatenate([nope, pe, pe_sw], axis=-1).reshape(lead + (PAIRS * B_PAIR_W,)).astype(bf16)


def _arrange_w_ukv(w):
    lead = w.shape[:2]
    w = w.reshape(lead + (HEADS, B_NOPE + HEAD_DIM))
    wk = w[..., :B_NOPE].reshape(lead + (BRANCH_W,))
    wv = w[..., B_NOPE:].reshape(lead + (BRANCH_W,))
    return wk.astype(bf16), wv.astype(bf16)


def _rel_bias_rows(rel_bias):
    lead = rel_bias.shape[:2]
    n_hi = A_WINDOW - ATT_T
    lo = jnp.broadcast_to(rel_bias[..., :1], lead + (REL_ROW - n_hi - rel_bias.shape[-1],))
    hi = jnp.broadcast_to(rel_bias[..., -1:], lead + (n_hi,))
    rows = jnp.concatenate([hi, rel_bias[..., ::-1], lo], axis=-1).astype(f32)
    return rows.reshape(lead + (1, REL_ROW))


def kernel(x, mem, positions, ln_mix_g, ln_mix_b, w_in, b_gate, b_forget, a_rel_bias, b_q_norm, b_kv_norm, b_w_uq, b_w_ukv, w_branch, w_mix_out, ln_xa_g, ln_xa_b, xa_w_q, xa_w_kv, xa_w_o, ln_ffn_g, ln_ffn_b, ffn_w_gu, ffn_w_down):
    xf = x.reshape(TOKENS, D_MODEL)
    memf = mem.reshape(BATCH * MEM_LEN, D_MODEL)
    pos = positions.reshape(TOKENS, 1)

    half = B_ROPE // 2
    inv_freq = ROPE_BASE ** (-jnp.arange(half, dtype=f32) / half)
    dead = jnp.zeros((PAIR_W - 2 * B_ROPE,), f32)
    freq_row = jnp.concatenate([jnp.tile(inv_freq, 4), dead]).reshape(1, PAIR_W)
    sign_row = jnp.concatenate([jnp.tile(jnp.concatenate([-jnp.ones((half,), f32), jnp.ones((half,), f32)]), 2),
                                dead]).reshape(1, PAIR_W)
    cos_t, sin_t = _rope_tables(pos, freq_row, sign_row)

    vec = lambda p: p.reshape(DEPTH, 1, p.shape[-1])
    wa, wc, wg, wl = _arrange_w_in(w_in)
    wq = _arrange_w_uq(b_w_uq)
    wk, wv = _arrange_w_ukv(b_w_ukv)
    rel_rows = _rel_bias_rows(a_rel_bias)
    w_branch_b, w_out_b = w_branch.astype(bf16), w_mix_out.astype(bf16)
    xa_q, xa_kv, xa_o = xa_w_q.astype(bf16), xa_w_kv.astype(bf16), xa_w_o.astype(bf16)
    w_gu, w_down = ffn_w_gu.astype(bf16), ffn_w_down.astype(bf16)
    bg = b_gate.reshape(DEPTH, 1, 3 * D_MODEL)
    bf = b_forget.reshape(DEPTH, HEADS, 1)

    for l in range(DEPTH):
        qkv_a, qkv_c, cf_t, q_b, k_b, v_b = _inproj(
            xf, wa, wc, wl, cos_t, sin_t, vec(b_q_norm), vec(b_kv_norm), wq, wk, wv, l)
        forget = _forget_cumsum(cf_t, bf, l)
        rel_table = _rel_table(rel_rows, l)

        y_a = _attention(qkv_a, qkv_a, qkv_a, rel_table, mode="rel", dq=PAIR_W, tq=ATT_T, pps=4,
                         kcol0=PAIRS, vcol0=2 * PAIRS)
        y_b = _attention(q_b, k_b, v_b, None, mode="mla", dq=B_PAIR_W, tq=CAUSAL_T, pps=4, kcol0=0, vcol0=0)
        y_c = _attention(qkv_c, qkv_c, qkv_c, forget, mode="fox", dq=PAIR_W, tq=CAUSAL_T, pps=2,
                         kcol0=PAIRS, vcol0=2 * PAIRS)

        xf = _merge(xf, y_a, y_b, y_c, wg, bg, w_branch_b, w_out_b, vec(ln_mix_g), vec(ln_mix_b), l)
        xf = _xattn(xf, memf, xa_q, xa_kv, xa_o, vec(ln_xa_g), vec(ln_xa_b), l)
        xf = _ffn(xf, w_gu, w_down, vec(ln_ffn_g), vec(ln_ffn_b), l)
    return xf.reshape(BATCH, SEQ, D_MODEL)
```

```python
import functools
import math

import jax
import jax.numpy as jnp
from jax import lax
from jax.experimental import pallas as pl
from jax.experimental.pallas import tpu as pltpu

D_MODEL = 1024
BATCH = 8
SEQ = 2048
DEPTH = 2
TOKENS = BATCH * SEQ
CHUNK = 64
MEM_LEN = 256

HEADS = 8
HEAD_DIM = 64
PAIRS = HEADS // 2
PAIR_W = 2 * HEAD_DIM
BRANCH_W = HEADS * HEAD_DIM

A_LEFT_CHUNKS = 8
ATT_T = 256
A_WINDOW = 3 * ATT_T
CAUSAL_T = 512

B_Q_LORA = 384
B_KV_LORA = 256
B_NOPE = 64
B_ROPE = 32
B_PAIR_W = 256
ROPE_BASE = 10000.0

XA_HEADS = 4
XA_HEAD_DIM = 128
FFN_HIDDEN = 2816
FFN_CHUNK = 256

LN_EPS = 1e-5
RMS_EPS = 1e-6
ALPHA = (2 * DEPTH) ** 0.25
LOG2E = math.log2(math.e)

QKV_W = 3 * BRANCH_W
LAT_CKV = B_Q_LORA
LAT_KR = B_Q_LORA + B_KV_LORA
LAT_W = LAT_KR + PAIR_W
CF_ROW0 = 2 * B_ROPE
REL_ROW = 1024

ROW_TILE = 1024
SUB_ROWS = 512

VMEM_LIMIT = 56 * 1024 * 1024

bf16 = jnp.bfloat16
f32 = jnp.float32


def _dot(a, b):
    return jnp.dot(a, b, preferred_element_type=f32)


def _dot_t(a, b):
    return lax.dot_general(a, b, (((1,), (1,)), ((), ())), preferred_element_type=f32)


def _layer_norm(z, g, b):
    mu = jnp.mean(z, axis=-1, keepdims=True)
    zc = z - mu
    var = jnp.mean(zc * zc, axis=-1, keepdims=True)
    return zc * lax.rsqrt(var + LN_EPS) * g + b


def _rms_norm(z, g):
    ms = jnp.mean(z * z, axis=-1, keepdims=True)
    return z * lax.rsqrt(ms + RMS_EPS) * g


def _params(*sem):
    return pltpu.CompilerParams(dimension_semantics=sem, vmem_limit_bytes=VMEM_LIMIT)


def _resident(shape):
    zeros = (0,) * len(shape)
    return pl.BlockSpec(shape, lambda *_: zeros, pipeline_mode=pl.Buffered(1))


def _layer(shape, l):
    zeros = (0,) * len(shape)
    return pl.BlockSpec((None,) + tuple(shape), lambda *_: (l,) + zeros, pipeline_mode=pl.Buffered(1))


def _inproj_kernel(x_ref, wa_ref, wc_ref, wl_ref, cos_ref, sin_ref, gq_ref, gkv_ref,
                   wq_ref, wk_ref, wv_ref,
                   qkva_ref, qkvc_ref, cft_ref, qb_ref, kb_ref, vb_ref):
    qk_scale = HEAD_DIM ** -0.5 * LOG2E
    scale = (B_NOPE + B_ROPE) ** -0.5 * LOG2E
    for r0 in range(0, x_ref.shape[0], SUB_ROWS):
        rows = slice(r0, r0 + SUB_ROWS)
        xb = x_ref[rows, :].astype(bf16)

        for dst, w_ref in ((qkva_ref, wa_ref), (qkvc_ref, wc_ref)):
            q, k, v = (slice(n * BRANCH_W, (n + 1) * BRANCH_W) for n in range(3))
            dst[rows, q] = (_dot_t(xb, w_ref[q, :]) * qk_scale).astype(bf16)
            dst[rows, k] = _dot_t(xb, w_ref[k, :]).astype(bf16)
            dst[rows, v] = _dot_t(xb, w_ref[v, :]).astype(bf16)

        kr_cf = _dot_t(xb, wl_ref[LAT_KR:LAT_W, :])
        cft_ref[:, rows] = kr_cf.T
        cos_t = cos_ref[rows, :]
        sin_t = sin_ref[rows, :]
        cqn = _rms_norm(_dot_t(xb, wl_ref[0:LAT_CKV, :]), gq_ref[...]).astype(bf16)
        q1 = _dot(cqn, wq_ref[...])
        ckvn = _rms_norm(_dot_t(xb, wl_ref[LAT_CKV:LAT_KR, :]), gkv_ref[...]).astype(bf16)
        kn = _dot(ckvn, wk_ref[...])
        vb_ref[rows, :] = _dot(ckvn, wv_ref[...]).astype(bf16)
        lane = lax.broadcasted_iota(jnp.int32, (1, PAIR_W), 1)
        kpe = jnp.where(lane < B_ROPE,
                        kr_cf * cos_t + pltpu.roll(kr_cf, PAIR_W - B_ROPE, 1) * sin_t,
                        pltpu.roll(kr_cf, B_ROPE, 1) * cos_t + kr_cf * sin_t).astype(bf16)
        for p in range(PAIRS):
            c0 = p * B_PAIR_W
            nope, rope_part = slice(c0, c0 + PAIR_W), slice(c0 + PAIR_W, c0 + B_PAIR_W)
            qb_ref[rows, nope] = (q1[:, nope] * scale).astype(bf16)
            blk = q1[:, rope_part]
            pe = blk * cos_t + pltpu.roll(blk, 2 * B_ROPE, 1) * sin_t
            qb_ref[rows, rope_part] = (pe * scale).astype(bf16)
            kb_ref[rows, nope] = kn[:, p * PAIR_W:(p + 1) * PAIR_W].astype(bf16)
            kb_ref[rows, rope_part] = kpe


def _inproj(x, wa, wc, wl, cos_t, sin_t, gq, gkv, wq, wk, wv, l):
    tm = ROW_TILE
    row = lambda i: (i, 0)
    return pl.pallas_call(
        _inproj_kernel,
        grid=(TOKENS // tm,),
        in_specs=[
            pl.BlockSpec((tm, D_MODEL), row),
            _layer((QKV_W, D_MODEL), l),
            _layer((QKV_W, D_MODEL), l),
            _layer((LAT_W, D_MODEL), l),
            pl.BlockSpec((tm, PAIR_W), row),
            pl.BlockSpec((tm, PAIR_W), row),
            _layer((1, B_Q_LORA), l),
            _layer((1, B_KV_LORA), l),
            _layer((B_Q_LORA, PAIRS * B_PAIR_W), l),
            _layer((B_KV_LORA, BRANCH_W), l),
            _layer((B_KV_LORA, BRANCH_W), l),
        ],
        out_specs=[
            pl.BlockSpec((tm, QKV_W), row),
            pl.BlockSpec((tm, QKV_W), row),
            pl.BlockSpec((PAIR_W, tm), lambda i: (0, i)),
            pl.BlockSpec((tm, PAIRS * B_PAIR_W), row),
            pl.BlockSpec((tm, PAIRS * B_PAIR_W), row),
            pl.BlockSpec((tm, BRANCH_W), row),
        ],
        out_shape=[
            jax.ShapeDtypeStruct((TOKENS, QKV_W), bf16),
            jax.ShapeDtypeStruct((TOKENS, QKV_W), bf16),
            jax.ShapeDtypeStruct((PAIR_W, TOKENS), f32),
            jax.ShapeDtypeStruct((TOKENS, PAIRS * B_PAIR_W), bf16),
            jax.ShapeDtypeStruct((TOKENS, PAIRS * B_PAIR_W), bf16),
            jax.ShapeDtypeStruct((TOKENS, BRANCH_W), bf16),
        ],
        compiler_params=_params("parallel"),
        name="inproj",
    )(x, wa, wc, wl, cos_t, sin_t, gq, gkv, wq, wk, wv)


def _rope_kernel(pos_ref, freq_ref, sign_ref, cos_ref, sin_ref):
    ang = pos_ref[...].astype(f32) * freq_ref[...]
    live = (freq_ref[...] > 0.0).astype(f32)
    cos_ref[...] = jnp.cos(ang) * live
    sin_ref[...] = jnp.sin(ang) * sign_ref[...]


def _rope_tables(pos, freq, sign):
    tm = 2048
    row = lambda i: (i, 0)
    return pl.pallas_call(
        _rope_kernel,
        grid=(TOKENS // tm,),
        in_specs=[pl.BlockSpec((tm, 1), row), _resident((1, PAIR_W)), _resident((1, PAIR_W))],
        out_specs=[pl.BlockSpec((tm, PAIR_W), row), pl.BlockSpec((tm, PAIR_W), row)],
        out_shape=[jax.ShapeDtypeStruct((TOKENS, PAIR_W), f32), jax.ShapeDtypeStruct((TOKENS, PAIR_W), f32)],
        compiler_params=_params("parallel"),
        name="rope_tables",
    )(pos, freq, sign)


def _forget_kernel(cf_ref, bf_ref, o_ref):
    rows = BATCH * HEADS
    lane = lax.broadcasted_iota(jnp.int32, (rows, 128), 1)
    bias = jnp.concatenate([bf_ref[...]] * BATCH, axis=0)
    carry = None
    for blk in range(SEQ // 128):
        z = jnp.concatenate([cf_ref[:, b * SEQ + blk * 128:b * SEQ + (blk + 1) * 128]
                             for b in range(BATCH)], axis=0) + bias
        acc = jnp.minimum(z, 0.0) - jnp.log(1.0 + jnp.exp(-jnp.abs(z)))
        d = 1
        while d < 128:
            acc = acc + jnp.where(lane >= d, pltpu.roll(acc, d, 1), 0.0)
            d *= 2
        if blk > 0:
            acc = acc + carry
        for b in range(BATCH):
            o_ref[:, b * SEQ + blk * 128:b * SEQ + (blk + 1) * 128] = acc[b * HEADS:(b + 1) * HEADS]
        carry = acc[:, 127:128]


def _forget_cumsum(cf_t, bf, l):
    return pl.pallas_call(
        _forget_kernel,
        grid=(1,),
        in_specs=[
            pl.BlockSpec((HEADS, TOKENS), lambda i: (CF_ROW0 // HEADS, 0)),
            _layer((HEADS, 1), l),
        ],
        out_specs=pl.BlockSpec((HEADS, TOKENS), lambda i: (0, 0)),
        out_shape=jax.ShapeDtypeStruct((HEADS, TOKENS), f32),
        compiler_params=_params("arbitrary"),
        name="forget_cumsum",
    )(cf_t, bf)


def _rel_table_kernel(w_ref, o_ref):
    x = jnp.broadcast_to(w_ref[...], (ATT_T, REL_ROW))
    x = pltpu.roll(x, REL_ROW - ATT_T, 1, stride=1, stride_axis=0)
    r = lax.broadcasted_iota(jnp.int32, (ATT_T, A_WINDOW), 0)
    c = lax.broadcasted_iota(jnp.int32, (ATT_T, A_WINDOW), 1)
    dchunk = (c >> 6) - (r >> 6)
    valid = (dchunk >= 0) & (dchunk <= A_LEFT_CHUNKS)
    o_ref[...] = jnp.where(valid, x[:, 0:A_WINDOW] * LOG2E, -jnp.inf)


def _rel_table(w_rows, l):
    return pl.pallas_call(
        _rel_table_kernel,
        grid=(HEADS,),
        in_specs=[pl.BlockSpec((None, None, 1, REL_ROW), lambda h: (l, h, 0, 0))],
        out_specs=pl.BlockSpec((None, ATT_T, A_WINDOW), lambda h: (h, 0, 0)),
        out_shape=jax.ShapeDtypeStruct((HEADS, ATT_T, A_WINDOW), f32),
        name="rel_table",
    )(w_rows)


def _pair_select(x, width):
    lane = lax.broadcasted_iota(jnp.int32, (1, width), 1)
    if width == PAIR_W:
        in_a = lane < HEAD_DIM
        in_b = lane >= HEAD_DIM
    else:
        in_a = (lane < 64) | ((lane >= 128) & (lane < 160))
        in_b = ((lane >= 64) & (lane < 128)) | ((lane >= 160) & (lane < 192))
    zero = jnp.zeros_like(x)
    return jnp.where(in_a, x, zero), jnp.where(in_b, x, zero)


def _attn_kernel(*refs, mode, dq, tq, pps):
    if mode == "mla":
        q_ref, k_ref, v_ref, o_ref = refs
        extra_ref = None
    else:
        q_ref, k_ref, v_ref, extra_ref, o_ref = refs
    i = pl.program_id(2)
    nq = SEQ // tq
    neg = jnp.float32(-jnp.inf)
    stack_pv = 2 * tq <= 512

    def attend(k0, nkeys, q0):
        lane = lax.broadcasted_iota(jnp.int32, (1, PAIR_W), 1)
        for p in range(pps):
            qa, qb = _pair_select(q_ref[:, p * dq:(p + 1) * dq], dq)
            kt = k_ref[pl.ds(k0, nkeys), p * dq:(p + 1) * dq]
            vt = v_ref[pl.ds(k0, nkeys), p * PAIR_W:(p + 1) * PAIR_W]
            s2 = _dot_t(jnp.concatenate([qa, qb], axis=0), kt)
            probs = []
            for head in range(2):
                s = s2[head * tq:(head + 1) * tq]
                if mode == "rel":
                    s = s + extra_ref[2 * p + head, :, A_WINDOW - nkeys:A_WINDOW]
                elif mode == "fox":
                    f = extra_ref[2 * p + head:2 * p + head + 1, :]
                    s = s + (f[:, q0:q0 + 1] - f[:, 0:nkeys]) * LOG2E
                if mode != "rel":
                    row = lax.broadcasted_iota(jnp.int32, (tq, tq), 0)
                    col = lax.broadcasted_iota(jnp.int32, (tq, tq), 1)
                    keep = (col <= row) if mode == "fox" else ((col >> 6) <= (row >> 6))
                    tail = jnp.where(keep, s[:, nkeys - tq:], neg)
                    s = tail if nkeys == tq else jnp.concatenate([s[:, :nkeys - tq], tail], axis=1)
                m = jnp.max(s, axis=-1, keepdims=True)
                probs.append(jnp.exp2(s - m).astype(bf16))
            v_ones = jnp.concatenate([vt, jnp.ones_like(vt)], axis=1)
            if stack_pv:
                o2 = _dot(jnp.concatenate(probs, axis=0), v_ones)
                oa, ob = o2[0:tq], o2[tq:]
            else:
                oa, ob = _dot(probs[0], v_ones), _dot(probs[1], v_ones)
            oa = oa[:, 0:PAIR_W] / oa[:, PAIR_W:]
            ob = ob[:, 0:PAIR_W] / ob[:, PAIR_W:]
            o_ref[:, p * PAIR_W:(p + 1) * PAIR_W] = jnp.where(lane < HEAD_DIM, oa, ob).astype(bf16)

    if mode == "rel":
        for ii in range(2):
            pl.when(i == ii)(functools.partial(attend, 0, (ii + 1) * tq, 0))

        @pl.when(i >= 2)
        def _():
            attend(pl.multiple_of((i - 2) * tq, tq), A_WINDOW, 0)
    else:
        for ii in range(nq):
            pl.when(i == ii)(functools.partial(attend, 0, (ii + 1) * tq, ii * tq))


def _attention(q, k, v, extra, *, mode, dq, tq, pps, kcol0, vcol0):
    nq = SEQ // tq
    groups = PAIRS // pps
    in_specs = [
        pl.BlockSpec((tq, pps * dq), lambda b, j, i: (b * nq + i, j)),
        pl.BlockSpec((SEQ, pps * dq), lambda b, j, i: (b, kcol0 // pps + j)),
        pl.BlockSpec((SEQ, pps * PAIR_W), lambda b, j, i: (b, vcol0 // pps + j)),
    ]
    args = [q, k, v]
    if mode == "fox":
        in_specs.append(pl.BlockSpec((None, 2 * pps, SEQ), lambda b, j, i: (j, 0, b)))
        args.append(extra.reshape(groups, 2 * pps, TOKENS))
    elif mode == "rel":
        assert tq == ATT_T
        in_specs.append(pl.BlockSpec((2 * pps, ATT_T, A_WINDOW), lambda b, j, i: (j, 0, 0)))
        args.append(extra)
    return pl.pallas_call(
        functools.partial(_attn_kernel, mode=mode, dq=dq, tq=tq, pps=pps),
        grid=(BATCH, groups, nq),
        in_specs=in_specs,
        out_specs=pl.BlockSpec((tq, pps * PAIR_W), lambda b, j, i: (b * nq + i, j)),
        out_shape=jax.ShapeDtypeStruct((TOKENS, BRANCH_W), bf16),
        compiler_params=_params("parallel", "parallel", "arbitrary"),
        name="attn_" + mode,
    )(*args)


def _merge_kernel(x_ref, ya_ref, yb_ref, yc_ref, wg_ref, bg_ref, wb_ref, wo_ref, g_ref, b_ref, o_ref):
    for r0 in range(0, x_ref.shape[0], SUB_ROWS):
        rows = slice(r0, r0 + SUB_ROWS)
        x = x_ref[rows, :]
        xb = x.astype(bf16)
        merged = None
        for n, y_ref in enumerate((ya_ref, yb_ref, yc_ref)):
            cols = slice(n * D_MODEL, (n + 1) * D_MODEL)
            z = _dot_t(xb, wg_ref[cols, :]) + bg_ref[:, cols]
            term = _dot(y_ref[rows, :], wb_ref[n]) / (1.0 + jnp.exp(-z))
            merged = term if merged is None else merged + term
        y = _dot(merged.astype(bf16), wo_ref[...])
        o_ref[rows, :] = _layer_norm(ALPHA * x + y, g_ref[...], b_ref[...])


def _merge(x, ya, yb, yc, wg, bg, wb, wo, g, b, l):
    tm = ROW_TILE
    row = lambda i: (i, 0)
    return pl.pallas_call(
        _merge_kernel,
        grid=(TOKENS // tm,),
        in_specs=[
            pl.BlockSpec((tm, D_MODEL), row),
            pl.BlockSpec((tm, BRANCH_W), row),
            pl.BlockSpec((tm, BRANCH_W), row),
            pl.BlockSpec((tm, BRANCH_W), row),
            _layer((3 * D_MODEL, D_MODEL), l),
            _layer((1, 3 * D_MODEL), l),
            _layer((3, BRANCH_W, D_MODEL), l),
            _layer((D_MODEL, D_MODEL), l),
            _layer((1, D_MODEL), l),
            _layer((1, D_MODEL), l),
        ],
        out_specs=pl.BlockSpec((tm, D_MODEL), row),
        out_shape=jax.ShapeDtypeStruct((TOKENS, D_MODEL), f32),
        compiler_params=_params("parallel"),
        name="merge",
    )(x, ya, yb, yc, wg, bg, wb, wo, g, b)


def _xattn_kernel(x_ref, mem_ref, wq_ref, wkv_ref, wo_ref, g_ref, b_ref, o_ref, kv_ref):
    @pl.when(pl.program_id(1) == 0)
    def _():
        kv_ref[...] = _dot(mem_ref[...].astype(bf16), wkv_ref[...]).astype(bf16)

    scale = XA_HEAD_DIM ** -0.5 * LOG2E
    width = XA_HEADS * XA_HEAD_DIM
    for r0 in range(0, x_ref.shape[0], SUB_ROWS):
        rows = slice(r0, r0 + SUB_ROWS)
        x = x_ref[rows, :]
        q = (_dot(x.astype(bf16), wq_ref[...]) * scale).astype(bf16)
        outs = []
        for h in range(XA_HEADS):
            c0 = h * XA_HEAD_DIM
            s = _dot_t(q[:, c0:c0 + XA_HEAD_DIM], kv_ref[:, c0:c0 + XA_HEAD_DIM])
            p = jnp.exp2(s - jnp.max(s, axis=-1, keepdims=True))
            l = jnp.sum(p, axis=-1, keepdims=True)
            o = _dot(p.astype(bf16), kv_ref[:, width + c0:width + c0 + XA_HEAD_DIM])
            outs.append((o / l).astype(bf16))
        y = _dot(jnp.concatenate(outs, axis=-1), wo_ref[...])
        o_ref[rows, :] = _layer_norm(ALPHA * x + y, g_ref[...], b_ref[...])


def _xattn(x, mem, wq, wkv, wo, g, b, l):
    tm = ROW_TILE
    nq = SEQ // tm
    width = XA_HEADS * XA_HEAD_DIM
    return pl.pallas_call(
        _xattn_kernel,
        grid=(BATCH, nq),
        in_specs=[
            pl.BlockSpec((tm, D_MODEL), lambda bi, i: (bi * nq + i, 0)),
            pl.BlockSpec((MEM_LEN, D_MODEL), lambda bi, i: (bi, 0)),
            _layer((D_MODEL, width), l),
            _layer((D_MODEL, 2 * width), l),
            _layer((width, D_MODEL), l),
            _layer((1, D_MODEL), l),
            _layer((1, D_MODEL), l),
        ],
        out_specs=pl.BlockSpec((tm, D_MODEL), lambda bi, i: (bi * nq + i, 0)),
        out_shape=jax.ShapeDtypeStruct((TOKENS, D_MODEL), f32),
        scratch_shapes=[pltpu.VMEM((MEM_LEN, 2 * width), bf16)],
        compiler_params=_params("arbitrary", "arbitrary"),
        name="xattn",
    )(x, mem, wq, wkv, wo, g, b)


def _ffn_kernel(x_ref, wgu_ref, wd_ref, g_ref, b_ref, o_ref, acc_ref):
    for r0 in range(0, x_ref.shape[0], SUB_ROWS):
        rows = slice(r0, r0 + SUB_ROWS)
        x = x_ref[rows, :]
        xb = x.astype(bf16)
        for c in range(FFN_HIDDEN // FFN_CHUNK):
            c0 = c * FFN_CHUNK
            gate = _dot(xb, wgu_ref[:, c0:c0 + FFN_CHUNK])
            up = _dot(xb, wgu_ref[:, FFN_HIDDEN + c0:FFN_HIDDEN + c0 + FFN_CHUNK])
            h = (gate / (1.0 + jnp.exp(-gate)) * up).astype(bf16)
            part = _dot(h, wd_ref[c0:c0 + FFN_CHUNK, :])
            if c == 0:
                acc_ref[rows, :] = part
            else:
                acc_ref[rows, :] += part
        o_ref[rows, :] = _layer_norm(ALPHA * x + acc_ref[rows, :], g_ref[...], b_ref[...])


def _ffn(x, wgu, wd, g, b, l):
    tm = ROW_TILE
    row = lambda i: (i, 0)
    return pl.pallas_call(
        _ffn_kernel,
        grid=(TOKENS // tm,),
        in_specs=[
            pl.BlockSpec((tm, D_MODEL), row),
            _layer((D_MODEL, 2 * FFN_HIDDEN), l),
            _layer((FFN_HIDDEN, D_MODEL), l),
            _layer((1, D_MODEL), l),
            _layer((1, D_MODEL), l),
        ],
        out_specs=pl.BlockSpec((tm, D_MODEL), row),
        out_shape=jax.ShapeDtypeStruct((TOKENS, D_MODEL), f32),
        scratch_shapes=[pltpu.VMEM((tm, D_MODEL), f32)],
        compiler_params=_params("parallel"),
        name="ffn",
    )(x, wgu, wd, g, b)


def _arrange_w_in(w):
    a0, b0 = 0, QKV_W
    c0 = b0 + B_Q_LORA + B_KV_LORA + B_ROPE
    f0 = c0 + QKV_W
    g0 = f0 + HEADS
    wt = jnp.swapaxes(w, 1, 2)
    a = wt[:, a0:b0].astype(bf16)
    c = wt[:, c0:f0].astype(bf16)
    g = wt[:, g0:g0 + 3 * D_MODEL].astype(bf16)
    small = wt[:, b0:c0].astype(bf16)
    cq_ckv = small[:, 0:LAT_KR]
    kr = small[:, LAT_KR:]
    cf = wt[:, f0:g0].astype(bf16)
    half = B_ROPE // 2
    kr_sw = jnp.concatenate([kr[:, half:], kr[:, :half]], axis=1)
    pad = jnp.zeros((w.shape[0], PAIR_W - 2 * B_ROPE - HEADS, w.shape[1]), bf16)
    lat = jnp.concatenate([cq_ckv, kr, kr_sw, cf, pad], axis=1)
    return a, c, g, lat


def _arrange_w_uq(w):
    lead = w.shape[:2]
    w = w.reshape(lead + (PAIRS, 2, B_NOPE + B_ROPE))
    nope = w[..., :B_NOPE].reshape(lead + (PAIRS, 2 * B_NOPE))
    pe = w[..., B_NOPE:]
    half = B_ROPE // 2
    pe_sw = jnp.concatenate([pe[..., half:], pe[..., :half]], axis=-1)
    pe = pe.reshape(lead + (PAIRS, 2 * B_ROPE))
    pe_sw = pe_sw.reshape(lead + (PAIRS, 2 * B_ROPE))
    return jnp.concatenate([nope, pe, pe_sw], axis=-1).reshape(lead + (PAIRS * B_PAIR_W,)).astype(bf16)


def _arrange_w_ukv(w):
    lead = w.shape[:2]
    w = w.reshape(lead + (HEADS, B_NOPE + HEAD_DIM))
    wk = w[..., :B_NOPE].reshape(lead + (BRANCH_W,))
    wv = w[..., B_NOPE:].reshape(lead + (BRANCH_W,))
    return wk.astype(bf16), wv.astype(bf16)


def _rel_bias_rows(rel_bias):
    lead = rel_bias.shape[:2]
    n_hi = A_WINDOW - ATT_T
    lo = jnp.broadcast_to(rel_bias[..., :1], lead + (REL_ROW - n_hi - rel_bias.shape[-1],))
    hi = jnp.broadcast_to(rel_bias[..., -1:], lead + (n_hi,))
    rows = jnp.concatenate([hi, rel_bias[..., ::-1], lo], axis=-1).astype(f32)
    return rows.reshape(lead + (1, REL_ROW))


def kernel(x, mem, positions, ln_mix_g, ln_mix_b, w_in, b_gate, b_forget, a_rel_bias, b_q_norm, b_kv_norm, b_w_uq, b_w_ukv, w_branch, w_mix_out, ln_xa_g, ln_xa_b, xa_w_q, xa_w_kv, xa_w_o, ln_ffn_g, ln_ffn_b, ffn_w_gu, ffn_w_down):
    xf = x.reshape(TOKENS, D_MODEL)
    memf = mem.reshape(BATCH * MEM_LEN, D_MODEL)
    pos = positions.reshape(TOKENS, 1)

    half = B_ROPE // 2
    inv_freq = ROPE_BASE ** (-jnp.arange(half, dtype=f32) / half)
    dead = jnp.zeros((PAIR_W - 2 * B_ROPE,), f32)
    freq_row = jnp.concatenate([jnp.tile(inv_freq, 4), dead]).reshape(1, PAIR_W)
    sign_row = jnp.concatenate([jnp.tile(jnp.concatenate([-jnp.ones((half,), f32), jnp.ones((half,), f32)]), 2),
                                dead]).reshape(1, PAIR_W)
    cos_t, sin_t = _rope_tables(pos, freq_row, sign_row)

    vec = lambda p: p.reshape(DEPTH, 1, p.shape[-1])
    wa, wc, wg, wl = _arrange_w_in(w_in)
    wq = _arrange_w_uq(b_w_uq)
    wk, wv = _arrange_w_ukv(b_w_ukv)
    rel_rows = _rel_bias_rows(a_rel_bias)
    w_branch_b, w_out_b = w_branch.astype(bf16), w_mix_out.astype(bf16)
    xa_q, xa_kv, xa_o = xa_w_q.astype(bf16), xa_w_kv.astype(bf16), xa_w_o.astype(bf16)
    w_gu, w_down = ffn_w_gu.astype(bf16), ffn_w_down.astype(bf16)
    bg = b_gate.reshape(DEPTH, 1, 3 * D_MODEL)
    bf = b_forget.reshape(DEPTH, HEADS, 1)

    for l in range(DEPTH):
        qkv_a, qkv_c, cf_t, q_b, k_b, v_b = _inproj(
            xf, wa, wc, wl, cos_t, sin_t, vec(b_q_norm), vec(b_kv_norm), wq, wk, wv, l)
        forget = _forget_cumsum(cf_t, bf, l)
        rel_table = _rel_table(rel_rows, l)

        y_a = _attention(qkv_a, qkv_a, qkv_a, rel_table, mode="rel", dq=PAIR_W, tq=ATT_T, pps=4,
                         kcol0=PAIRS, vcol0=2 * PAIRS)
        y_b = _attention(q_b, k_b, v_b, None, mode="mla", dq=B_PAIR_W, tq=CAUSAL_T, pps=2, kcol0=0, vcol0=0)
        y_c = _attention(qkv_c, qkv_c, qkv_c, forget, mode="fox", dq=PAIR_W, tq=CAUSAL_T, pps=2,
                         kcol0=PAIRS, vcol0=2 * PAIRS)

        xf = _merge(xf, y_a, y_b, y_c, wg, bg, w_branch_b, w_out_b, vec(ln_mix_g), vec(ln_mix_b), l)
        xf = _xattn(xf, memf, xa_q, xa_kv, xa_o, vec(ln_xa_g), vec(ln_xa_b), l)
        xf = _ffn(xf, w_gu, w_down, vec(ln_ffn_g), vec(ln_ffn_b), l)
    return xf.reshape(BATCH, SEQ, D_MODEL)
```

```python
import functools
import math

import jax
import jax.numpy as jnp
from jax import lax
from jax.experimental import pallas as pl
from jax.experimental.pallas import tpu as pltpu

D_MODEL = 1024
BATCH = 8
SEQ = 2048
DEPTH = 2
TOKENS = BATCH * SEQ
CHUNK = 64
CHUNK_SHIFT = CHUNK.bit_length() - 1
MEM_LEN = 256

HEADS = 8
HEAD_DIM = 64
PAIRS = HEADS // 2
PAIR_W = 2 * HEAD_DIM
BRANCH_W = HEADS * HEAD_DIM

A_LEFT_CHUNKS = 8
ATT_T = 256
A_WINDOW = 3 * ATT_T
CAUSAL_T = 512

B_Q_LORA = 384
B_KV_LORA = 256
B_NOPE = 64
B_ROPE = 32
B_PAIR_W = 256
ROPE_BASE = 10000.0

XA_HEADS = 4
XA_HEAD_DIM = 128
FFN_HIDDEN = 2816
FFN_CHUNK = 256

LN_EPS = 1e-5
RMS_EPS = 1e-6
ALPHA = (2 * DEPTH) ** 0.25
LOG2E = math.log2(math.e)

QKV_W = 3 * BRANCH_W
LAT_CKV = B_Q_LORA
LAT_KR = B_Q_LORA + B_KV_LORA
LAT_W = LAT_KR + PAIR_W
CF_ROW0 = 2 * B_ROPE
REL_ROW = 1024

ROW_TILE = 1024
SUB_ROWS = 512

VMEM_LIMIT = 56 * 1024 * 1024

bf16 = jnp.bfloat16
f32 = jnp.float32


def _dot(a, b):
    return jnp.dot(a, b, preferred_element_type=f32)


def _dot_t(a, b):
    return lax.dot_general(a, b, (((1,), (1,)), ((), ())), preferred_element_type=f32)


def _layer_norm(z, g, b):
    mu = jnp.mean(z, axis=-1, keepdims=True)
    zc = z - mu
    var = jnp.mean(zc * zc, axis=-1, keepdims=True)
    return zc * lax.rsqrt(var + LN_EPS) * g + b


def _rms_norm(z, g):
    ms = jnp.mean(z * z, axis=-1, keepdims=True)
    return z * lax.rsqrt(ms + RMS_EPS) * g


def _params(*sem):
    return pltpu.CompilerParams(dimension_semantics=sem, vmem_limit_bytes=VMEM_LIMIT)


def _resident(shape):
    zeros = (0,) * len(shape)
    return pl.BlockSpec(shape, lambda *_: zeros, pipeline_mode=pl.Buffered(1))


def _layer(shape, l):
    zeros = (0,) * len(shape)
    return pl.BlockSpec((None,) + tuple(shape), lambda *_: (l,) + zeros, pipeline_mode=pl.Buffered(1))


def _inproj_kernel(x_ref, wa_ref, wc_ref, wl_ref, cos_ref, sin_ref, gq_ref, gkv_ref,
                   wq_ref, wk_ref, wv_ref,
                   qkva_ref, qkvc_ref, cft_ref, qb_ref, kb_ref, vb_ref):
    qk_scale = HEAD_DIM ** -0.5 * LOG2E
    scale = (B_NOPE + B_ROPE) ** -0.5 * LOG2E
    for r0 in range(0, x_ref.shape[0], SUB_ROWS):
        rows = slice(r0, r0 + SUB_ROWS)
        xb = x_ref[rows, :].astype(bf16)

        for dst, w_ref in ((qkva_ref, wa_ref), (qkvc_ref, wc_ref)):
            q, k, v = (slice(n * BRANCH_W, (n + 1) * BRANCH_W) for n in range(3))
            dst[rows, q] = (_dot_t(xb, w_ref[q, :]) * qk_scale).astype(bf16)
            dst[rows, k] = _dot_t(xb, w_ref[k, :]).astype(bf16)
            dst[rows, v] = _dot_t(xb, w_ref[v, :]).astype(bf16)

        kr_cf = _dot_t(xb, wl_ref[LAT_KR:LAT_W, :])
        cft_ref[:, rows] = kr_cf.T
        cos_t = cos_ref[rows, :]
        sin_t = sin_ref[rows, :]
        cqn = _rms_norm(_dot_t(xb, wl_ref[0:LAT_CKV, :]), gq_ref[...]).astype(bf16)
        q1 = _dot(cqn, wq_ref[...])
        ckvn = _rms_norm(_dot_t(xb, wl_ref[LAT_CKV:LAT_KR, :]), gkv_ref[...]).astype(bf16)
        kn = _dot(ckvn, wk_ref[...])
        vb_ref[rows, :] = _dot(ckvn, wv_ref[...]).astype(bf16)
        lane = lax.broadcasted_iota(jnp.int32, (1, PAIR_W), 1)
        kpe = jnp.where(lane < B_ROPE,
                        kr_cf * cos_t + pltpu.roll(kr_cf, PAIR_W - B_ROPE, 1) * sin_t,
                        pltpu.roll(kr_cf, B_ROPE, 1) * cos_t + kr_cf * sin_t).astype(bf16)
        for p in range(PAIRS):
            c0 = p * B_PAIR_W
            nope, rope_part = slice(c0, c0 + PAIR_W), slice(c0 + PAIR_W, c0 + B_PAIR_W)
            qb_ref[rows, nope] = (q1[:, nope] * scale).astype(bf16)
            blk = q1[:, rope_part]
            pe = blk * cos_t + pltpu.roll(blk, 2 * B_ROPE, 1) * sin_t
            qb_ref[rows, rope_part] = (pe * scale).astype(bf16)
            kb_ref[rows, nope] = kn[:, p * PAIR_W:(p + 1) * PAIR_W].astype(bf16)
            kb_ref[rows, rope_part] = kpe


def _inproj(x, wa, wc, wl, cos_t, sin_t, gq, gkv, wq, wk, wv, l):
    tm = ROW_TILE
    row = lambda i: (i, 0)
    return pl.pallas_call(
        _inproj_kernel,
        grid=(TOKENS // tm,),
        in_specs=[
            pl.BlockSpec((tm, D_MODEL), row),
            _layer((QKV_W, D_MODEL), l),
            _layer((QKV_W, D_MODEL), l),
            _layer((LAT_W, D_MODEL), l),
            pl.BlockSpec((tm, PAIR_W), row),
            pl.BlockSpec((tm, PAIR_W), row),
            _layer((1, B_Q_LORA), l),
            _layer((1, B_KV_LORA), l),
            _layer((B_Q_LORA, PAIRS * B_PAIR_W), l),
            _layer((B_KV_LORA, BRANCH_W), l),
            _layer((B_KV_LORA, BRANCH_W), l),
        ],
        out_specs=[
            pl.BlockSpec((tm, QKV_W), row),
            pl.BlockSpec((tm, QKV_W), row),
            pl.BlockSpec((PAIR_W, tm), lambda i: (0, i)),
            pl.BlockSpec((tm, PAIRS * B_PAIR_W), row),
            pl.BlockSpec((tm, PAIRS * B_PAIR_W), row),
            pl.BlockSpec((tm, BRANCH_W), row),
        ],
        out_shape=[
            jax.ShapeDtypeStruct((TOKENS, QKV_W), bf16),
            jax.ShapeDtypeStruct((TOKENS, QKV_W), bf16),
            jax.ShapeDtypeStruct((PAIR_W, TOKENS), f32),
            jax.ShapeDtypeStruct((TOKENS, PAIRS * B_PAIR_W), bf16),
            jax.ShapeDtypeStruct((TOKENS, PAIRS * B_PAIR_W), bf16),
            jax.ShapeDtypeStruct((TOKENS, BRANCH_W), bf16),
        ],
        compiler_params=_params("parallel"),
        name="inproj",
    )(x, wa, wc, wl, cos_t, sin_t, gq, gkv, wq, wk, wv)


def _rope_kernel(pos_ref, freq_ref, sign_ref, cos_ref, sin_ref):
    ang = pos_ref[...].astype(f32) * freq_ref[...]
    live = (freq_ref[...] > 0.0).astype(f32)
    cos_ref[...] = jnp.cos(ang) * live
    sin_ref[...] = jnp.sin(ang) * sign_ref[...]


def _rope_tables(pos, freq, sign):
    tm = 2048
    row = lambda i: (i, 0)
    return pl.pallas_call(
        _rope_kernel,
        grid=(TOKENS // tm,),
        in_specs=[pl.BlockSpec((tm, 1), row), _resident((1, PAIR_W)), _resident((1, PAIR_W))],
        out_specs=[pl.BlockSpec((tm, PAIR_W), row), pl.BlockSpec((tm, PAIR_W), row)],
        out_shape=[jax.ShapeDtypeStruct((TOKENS, PAIR_W), f32), jax.ShapeDtypeStruct((TOKENS, PAIR_W), f32)],
        compiler_params=_params("parallel"),
        name="rope_tables",
    )(pos, freq, sign)


def _forget_kernel(cf_ref, bf_ref, o_ref):
    rows = BATCH * HEADS
    lane = lax.broadcasted_iota(jnp.int32, (rows, 128), 1)
    bias = jnp.concatenate([bf_ref[...]] * BATCH, axis=0)
    carry = None
    for blk in range(SEQ // 128):
        z = jnp.concatenate([cf_ref[:, b * SEQ + blk * 128:b * SEQ + (blk + 1) * 128]
                             for b in range(BATCH)], axis=0) + bias
        acc = jnp.minimum(z, 0.0) - jnp.log(1.0 + jnp.exp(-jnp.abs(z)))
        d = 1
        while d < 128:
            acc = acc + jnp.where(lane >= d, pltpu.roll(acc, d, 1), 0.0)
            d *= 2
        if blk > 0:
            acc = acc + carry
        for b in range(BATCH):
            o_ref[:, b * SEQ + blk * 128:b * SEQ + (blk + 1) * 128] = acc[b * HEADS:(b + 1) * HEADS]
        carry = acc[:, 127:128]


def _forget_cumsum(cf_t, bf, l):
    return pl.pallas_call(
        _forget_kernel,
        grid=(1,),
        in_specs=[
            pl.BlockSpec((HEADS, TOKENS), lambda i: (CF_ROW0 // HEADS, 0)),
            _layer((HEADS, 1), l),
        ],
        out_specs=pl.BlockSpec((HEADS, TOKENS), lambda i: (0, 0)),
        out_shape=jax.ShapeDtypeStruct((HEADS, TOKENS), f32),
        compiler_params=_params("arbitrary"),
        name="forget_cumsum",
    )(cf_t, bf)


def _rel_table_kernel(w_ref, o_ref):
    x = jnp.broadcast_to(w_ref[...], (ATT_T, REL_ROW))
    x = pltpu.roll(x, REL_ROW - ATT_T, 1, stride=1, stride_axis=0)
    r = lax.broadcasted_iota(jnp.int32, (ATT_T, A_WINDOW), 0)
    c = lax.broadcasted_iota(jnp.int32, (ATT_T, A_WINDOW), 1)
    dchunk = (c >> CHUNK_SHIFT) - (r >> CHUNK_SHIFT)
    valid = (dchunk >= 0) & (dchunk <= A_LEFT_CHUNKS)
    o_ref[...] = jnp.where(valid, x[:, 0:A_WINDOW] * LOG2E, -jnp.inf)


def _rel_table(w_rows, l):
    return pl.pallas_call(
        _rel_table_kernel,
        grid=(HEADS,),
        in_specs=[pl.BlockSpec((None, None, 1, REL_ROW), lambda h: (l, h, 0, 0))],
        out_specs=pl.BlockSpec((None, ATT_T, A_WINDOW), lambda h: (h, 0, 0)),
        out_shape=jax.ShapeDtypeStruct((HEADS, ATT_T, A_WINDOW), f32),
        name="rel_table",
    )(w_rows)


def _pair_select(x, width):
    lane = lax.broadcasted_iota(jnp.int32, (1, width), 1)
    if width == PAIR_W:
        in_a = lane < HEAD_DIM
        in_b = lane >= HEAD_DIM
    else:
        in_a = (lane < 64) | ((lane >= 128) & (lane < 160))
        in_b = ((lane >= 64) & (lane < 128)) | ((lane >= 160) & (lane < 192))
    zero = jnp.zeros_like(x)
    return jnp.where(in_a, x, zero), jnp.where(in_b, x, zero)


def _attn_kernel(*refs, mode, dq, tq, pps):
    if mode == "mla":
        q_ref, k_ref, v_ref, o_ref = refs
        extra_ref = None
    else:
        q_ref, k_ref, v_ref, extra_ref, o_ref = refs
    i = pl.program_id(2)
    nq = SEQ // tq
    neg = jnp.float32(-jnp.inf)
    stack_pv = 2 * tq <= 512

    def attend(k0, nkeys, q0):
        lane = lax.broadcasted_iota(jnp.int32, (1, PAIR_W), 1)
        for p in range(pps):
            qa, qb = _pair_select(q_ref[:, p * dq:(p + 1) * dq], dq)
            kt = k_ref[pl.ds(k0, nkeys), p * dq:(p + 1) * dq]
            vt = v_ref[pl.ds(k0, nkeys), p * PAIR_W:(p + 1) * PAIR_W]
            s2 = _dot_t(jnp.concatenate([qa, qb], axis=0), kt)
            probs = []
            for head in range(2):
                s = s2[head * tq:(head + 1) * tq]
                if mode == "rel":
                    s = s + extra_ref[2 * p + head, :, A_WINDOW - nkeys:A_WINDOW]
                elif mode == "fox":
                    f = extra_ref[2 * p + head:2 * p + head + 1, :]
                    s = s + (f[:, q0:q0 + 1] - f[:, 0:nkeys]) * LOG2E
                if mode != "rel":
                    row = lax.broadcasted_iota(jnp.int32, (tq, tq), 0)
                    col = lax.broadcasted_iota(jnp.int32, (tq, tq), 1)
                    keep = (col <= row) if mode == "fox" else ((col >> CHUNK_SHIFT) <= (row >> CHUNK_SHIFT))
                    tail = jnp.where(keep, s[:, nkeys - tq:], neg)
                    s = tail if nkeys == tq else jnp.concatenate([s[:, :nkeys - tq], tail], axis=1)
                m = jnp.max(s, axis=-1, keepdims=True)
                probs.append(jnp.exp2(s - m).astype(bf16))
            v_ones = jnp.concatenate([vt, jnp.ones_like(vt)], axis=1)
            if stack_pv:
                o2 = _dot(jnp.concatenate(probs, axis=0), v_ones)
                oa, ob = o2[0:tq], o2[tq:]
            else:
                oa, ob = _dot(probs[0], v_ones), _dot(probs[1], v_ones)
            oa = oa[:, 0:PAIR_W] / oa[:, PAIR_W:]
            ob = ob[:, 0:PAIR_W] / ob[:, PAIR_W:]
            o_ref[:, p * PAIR_W:(p + 1) * PAIR_W] = jnp.where(lane < HEAD_DIM, oa, ob).astype(bf16)

    if mode == "rel":
        for ii in range(2):
            pl.when(i == ii)(functools.partial(attend, 0, (ii + 1) * tq, 0))

        @pl.when(i >= 2)
        def _():
            attend(pl.multiple_of((i - 2) * tq, tq), A_WINDOW, 0)
    else:
        for ii in range(nq):
            pl.when(i == ii)(functools.partial(attend, 0, (ii + 1) * tq, ii * tq))


def _attention(q, k, v, extra, *, mode, dq, tq, pps, kcol0, vcol0):
    nq = SEQ // tq
    groups = PAIRS // pps
    in_specs = [
        pl.BlockSpec((tq, pps * dq), lambda b, j, i: (b * nq + i, j)),
        pl.BlockSpec((SEQ, pps * dq), lambda b, j, i: (b, kcol0 // pps + j)),
        pl.BlockSpec((SEQ, pps * PAIR_W), lambda b, j, i: (b, vcol0 // pps + j)),
    ]
    args = [q, k, v]
    if mode == "fox":
        in_specs.append(pl.BlockSpec((None, 2 * pps, SEQ), lambda b, j, i: (j, 0, b)))
        args.append(extra.reshape(groups, 2 * pps, TOKENS))
    elif mode == "rel":
        assert tq == ATT_T
        in_specs.append(pl.BlockSpec((2 * pps, ATT_T, A_WINDOW), lambda b, j, i: (j, 0, 0)))
        args.append(extra)
    return pl.pallas_call(
        functools.partial(_attn_kernel, mode=mode, dq=dq, tq=tq, pps=pps),
        grid=(BATCH, groups, nq),
        in_specs=in_specs,
        out_specs=pl.BlockSpec((tq, pps * PAIR_W), lambda b, j, i: (b * nq + i, j)),
        out_shape=jax.ShapeDtypeStruct((TOKENS, BRANCH_W), bf16),
        compiler_params=_params("parallel", "parallel", "arbitrary"),
        name="attn_" + mode,
    )(*args)


def _merge_kernel(x_ref, ya_ref, yb_ref, yc_ref, wg_ref, bg_ref, wb_ref, wo_ref, g_ref, b_ref, o_ref):
    for r0 in range(0, x_ref.shape[0], SUB_ROWS):
        rows = slice(r0, r0 + SUB_ROWS)
        x = x_ref[rows, :]
        xb = x.astype(bf16)
        merged = None
        for n, y_ref in enumerate((ya_ref, yb_ref, yc_ref)):
            cols = slice(n * D_MODEL, (n + 1) * D_MODEL)
            z = _dot_t(xb, wg_ref[cols, :]) + bg_ref[:, cols]
            term = _dot(y_ref[rows, :], wb_ref[n]) / (1.0 + jnp.exp(-z))
            merged = term if merged is None else merged + term
        y = _dot(merged.astype(bf16), wo_ref[...])
        o_ref[rows, :] = _layer_norm(ALPHA * x + y, g_ref[...], b_ref[...])


def _merge(x, ya, yb, yc, wg, bg, wb, wo, g, b, l):
    tm = ROW_TILE
    row = lambda i: (i, 0)
    return pl.pallas_call(
        _merge_kernel,
        grid=(TOKENS // tm,),
        in_specs=[
            pl.BlockSpec((tm, D_MODEL), row),
            pl.BlockSpec((tm, BRANCH_W), row),
            pl.BlockSpec((tm, BRANCH_W), row),
            pl.BlockSpec((tm, BRANCH_W), row),
            _layer((3 * D_MODEL, D_MODEL), l),
            _layer((1, 3 * D_MODEL), l),
            _layer((3, BRANCH_W, D_MODEL), l),
            _layer((D_MODEL, D_MODEL), l),
            _layer((1, D_MODEL), l),
            _layer((1, D_MODEL), l),
        ],
        out_specs=pl.BlockSpec((tm, D_MODEL), row),
        out_shape=jax.ShapeDtypeStruct((TOKENS, D_MODEL), f32),
        compiler_params=_params("parallel"),
        name="merge",
    )(x, ya, yb, yc, wg, bg, wb, wo, g, b)


def _xattn_kernel(x_ref, mem_ref, wq_ref, wkv_ref, wo_ref, g_ref, b_ref, o_ref, kv_ref):
    @pl.when(pl.program_id(1) == 0)
    def _():
        kv_ref[...] = _dot(mem_ref[...].astype(bf16), wkv_ref[...]).astype(bf16)

    scale = XA_HEAD_DIM ** -0.5 * LOG2E
    width = XA_HEADS * XA_HEAD_DIM
    for r0 in range(0, x_ref.shape[0], SUB_ROWS):
        rows = slice(r0, r0 + SUB_ROWS)
        x = x_ref[rows, :]
        q = (_dot(x.astype(bf16), wq_ref[...]) * scale).astype(bf16)
        outs = []
        for h in range(XA_HEADS):
            c0 = h * XA_HEAD_DIM
            s = _dot_t(q[:, c0:c0 + XA_HEAD_DIM], kv_ref[:, c0:c0 + XA_HEAD_DIM])
            p = jnp.exp2(s - jnp.max(s, axis=-1, keepdims=True))
            l = jnp.sum(p, axis=-1, keepdims=True)
            o = _dot(p.astype(bf16), kv_ref[:, width + c0:width + c0 + XA_HEAD_DIM])
            outs.append((o / l).astype(bf16))
        y = _dot(jnp.concatenate(outs, axis=-1), wo_ref[...])
        o_ref[rows, :] = _layer_norm(ALPHA * x + y, g_ref[...], b_ref[...])


def _xattn(x, mem, wq, wkv, wo, g, b, l):
    tm = ROW_TILE
    nq = SEQ // tm
    width = XA_HEADS * XA_HEAD_DIM
    return pl.pallas_call(
        _xattn_kernel,
        grid=(BATCH, nq),
        in_specs=[
            pl.BlockSpec((tm, D_MODEL), lambda bi, i: (bi * nq + i, 0)),
            pl.BlockSpec((MEM_LEN, D_MODEL), lambda bi, i: (bi, 0)),
            _layer((D_MODEL, width), l),
            _layer((D_MODEL, 2 * width), l),
            _layer((width, D_MODEL), l),
            _layer((1, D_MODEL), l),
            _layer((1, D_MODEL), l),
        ],
        out_specs=pl.BlockSpec((tm, D_MODEL), lambda bi, i: (bi * nq + i, 0)),
        out_shape=jax.ShapeDtypeStruct((TOKENS, D_MODEL), f32),
        scratch_shapes=[pltpu.VMEM((MEM_LEN, 2 * width), bf16)],
        compiler_params=_params("arbitrary", "arbitrary"),
        name="xattn",
    )(x, mem, wq, wkv, wo, g, b)


def _ffn_kernel(x_ref, wgu_ref, wd_ref, g_ref, b_ref, o_ref, acc_ref):
    for r0 in range(0, x_ref.shape[0], SUB_ROWS):
        rows = slice(r0, r0 + SUB_ROWS)
        x = x_ref[rows, :]
        xb = x.astype(bf16)
        for c in range(FFN_HIDDEN // FFN_CHUNK):
            c0 = c * FFN_CHUNK
            gate = _dot(xb, wgu_ref[:, c0:c0 + FFN_CHUNK])
            up = _dot(xb, wgu_ref[:, FFN_HIDDEN + c0:FFN_HIDDEN + c0 + FFN_CHUNK])
            h = (gate / (1.0 + jnp.exp(-gate)) * up).astype(bf16)
            part = _dot(h, wd_ref[c0:c0 + FFN_CHUNK, :])
            if c == 0:
                acc_ref[rows, :] = part
            else:
                acc_ref[rows, :] += part
        o_ref[rows, :] = _layer_norm(ALPHA * x + acc_ref[rows, :], g_ref[...], b_ref[...])


def _ffn(x, wgu, wd, g, b, l):
    tm = ROW_TILE
    row = lambda i: (i, 0)
    return pl.pallas_call(
        _ffn_kernel,
        grid=(TOKENS // tm,),
        in_specs=[
            pl.BlockSpec((tm, D_MODEL), row),
            _layer((D_MODEL, 2 * FFN_HIDDEN), l),
            _layer((FFN_HIDDEN, D_MODEL), l),
            _layer((1, D_MODEL), l),
            _layer((1, D_MODEL), l),
        ],
        out_specs=pl.BlockSpec((tm, D_MODEL), row),
        out_shape=jax.ShapeDtypeStruct((TOKENS, D_MODEL), f32),
        scratch_shapes=[pltpu.VMEM((tm, D_MODEL), f32)],
        compiler_params=_params("parallel"),
        name="ffn",
    )(x, wgu, wd, g, b)


def _arrange_w_in(w):
    a0, b0 = 0, QKV_W
    c0 = b0 + B_Q_LORA + B_KV_LORA + B_ROPE
    f0 = c0 + QKV_W
    g0 = f0 + HEADS
    wt = jnp.swapaxes(w, 1, 2)
    a = wt[:, a0:b0].astype(bf16)
    c = wt[:, c0:f0].astype(bf16)
    g = wt[:, g0:g0 + 3 * D_MODEL].astype(bf16)
    small = wt[:, b0:c0].astype(bf16)
    cq_ckv = small[:, 0:LAT_KR]
    kr = small[:, LAT_KR:]
    cf = wt[:, f0:g0].astype(bf16)
    half = B_ROPE // 2
    kr_sw = jnp.concatenate([kr[:, half:], kr[:, :half]], axis=1)
    pad = jnp.zeros((w.shape[0], PAIR_W - 2 * B_ROPE - HEADS, w.shape[1]), bf16)
    lat = jnp.concatenate([cq_ckv, kr, kr_sw, cf, pad], axis=1)
    return a, c, g, lat


def _arrange_w_uq(w):
    lead = w.shape[:2]
    w = w.reshape(lead + (PAIRS, 2, B_NOPE + B_ROPE))
    nope = w[..., :B_NOPE].reshape(lead + (PAIRS, 2 * B_NOPE))
    pe = w[..., B_NOPE:]
    half = B_ROPE // 2
    pe_sw = jnp.concatenate([pe[..., half:], pe[..., :half]], axis=-1)
    pe = pe.reshape(lead + (PAIRS, 2 * B_ROPE))
    pe_sw = pe_sw.reshape(lead + (PAIRS, 2 * B_ROPE))
    return jnp.concatenate([nope, pe, pe_sw], axis=-1).reshape(lead + (PAIRS * B_PAIR_W,)).astype(bf16)


def _arrange_w_ukv(w):
    lead = w.shape[:2]
    w = w.reshape(lead + (HEADS, B_NOPE + HEAD_DIM))
    wk = w[..., :B_NOPE].reshape(lead + (BRANCH_W,))
    wv = w[..., B_NOPE:].reshape(lead + (BRANCH_W,))
    return wk.astype(bf16), wv.astype(bf16)


def _rel_bias_rows(rel_bias):
    lead = rel_bias.shape[:2]
    n_hi = A_WINDOW - ATT_T
    lo = jnp.broadcast_to(rel_bias[..., :1], lead + (REL_ROW - n_hi - rel_bias.shape[-1],))
    hi = jnp.broadcast_to(rel_bias[..., -1:], lead + (n_hi,))
    rows = jnp.concatenate([hi, rel_bias[..., ::-1], lo], axis=-1).astype(f32)
    return rows.reshape(lead + (1, REL_ROW))


def kernel(x, mem, positions, ln_mix_g, ln_mix_b, w_in, b_gate, b_forget, a_rel_bias, b_q_norm, b_kv_norm, b_w_uq, b_w_ukv, w_branch, w_mix_out, ln_xa_g, ln_xa_b, xa_w_q, xa_w_kv, xa_w_o, ln_ffn_g, ln_ffn_b, ffn_w_gu, ffn_w_down):
    xf = x.reshape(TOKENS, D_MODEL)
    memf = mem.reshape(BATCH * MEM_LEN, D_MODEL)
    pos = positions.reshape(TOKENS, 1)

    half = B_ROPE // 2
    inv_freq = ROPE_BASE ** (-jnp.arange(half, dtype=f32) / half)
    dead = jnp.zeros((PAIR_W - 2 * B_ROPE,), f32)
    freq_row = jnp.concatenate([jnp.tile(inv_freq, 4), dead]).reshape(1, PAIR_W)
    sign_row = jnp.concatenate([jnp.tile(jnp.concatenate([-jnp.ones((half,), f32), jnp.ones((half,), f32)]), 2),
                                dead]).reshape(1, PAIR_W)
    cos_t, sin_t = _rope_tables(pos, freq_row, sign_row)

    vec = lambda p: p.reshape(DEPTH, 1, p.shape[-1])
    wa, wc, wg, wl = _arrange_w_in(w_in)
    wq = _arrange_w_uq(b_w_uq)
    wk, wv = _arrange_w_ukv(b_w_ukv)
    rel_rows = _rel_bias_rows(a_rel_bias)
    w_branch_b, w_out_b = w_branch.astype(bf16), w_mix_out.astype(bf16)
    xa_q, xa_kv, xa_o = xa_w_q.astype(bf16), xa_w_kv.astype(bf16), xa_w_o.astype(bf16)
    w_gu, w_down = ffn_w_gu.astype(bf16), ffn_w_down.astype(bf16)
    bg = b_gate.reshape(DEPTH, 1, 3 * D_MODEL)
    bf = b_forget.reshape(DEPTH, HEADS, 1)

    for l in range(DEPTH):
        qkv_a, qkv_c, cf_t, q_b, k_b, v_b = _inproj(
            xf, wa, wc, wl, cos_t, sin_t, vec(b_q_norm), vec(b_kv_norm), wq, wk, wv, l)
        forget = _forget_cumsum(cf_t, bf, l)
        rel_table = _rel_table(rel_rows, l)

        y_a = _attention(qkv_a, qkv_a, qkv_a, rel_table, mode="rel", dq=PAIR_W, tq=ATT_T, pps=4,
                         kcol0=PAIRS, vcol0=2 * PAIRS)
        y_b = _attention(q_b, k_b, v_b, None, mode="mla", dq=B_PAIR_W, tq=CAUSAL_T, pps=2, kcol0=0, vcol0=0)
        y_c = _attention(qkv_c, qkv_c, qkv_c, forget, mode="fox", dq=PAIR_W, tq=CAUSAL_T, pps=2,
                         kcol0=PAIRS, vcol0=2 * PAIRS)

        xf = _merge(xf, y_a, y_b, y_c, wg, bg, w_branch_b, w_out_b, vec(ln_mix_g), vec(ln_mix_b), l)
        xf = _xattn(xf, memf, xa_q, xa_kv, xa_o, vec(ln_xa_g), vec(ln_xa_b), l)
        xf = _ffn(xf, w_gu, w_down, vec(ln_ffn_g), vec(ln_ffn_b), l)
    return xf.reshape(BATCH, SEQ, D_MODEL)
```

```python
import functools
import math

import jax
import jax.numpy as jnp
from jax import lax
from jax.experimental import pallas as pl
from jax.experimental.pallas import tpu as pltpu

D_MODEL = 1024
BATCH = 8
SEQ = 2048
DEPTH = 2
TOKENS = BATCH * SEQ
CHUNK = 64
CHUNK_SHIFT = CHUNK.bit_length() - 1
MEM_LEN = 256

HEADS = 8
HEAD_DIM = 64
PAIRS = HEADS // 2
PAIR_W = 2 * HEAD_DIM
BRANCH_W = HEADS * HEAD_DIM

A_LEFT_CHUNKS = 8
ATT_T = 256
A_WINDOW = 3 * ATT_T
CAUSAL_T = 512

B_Q_LORA = 384
B_KV_LORA = 256
B_NOPE = 64
B_ROPE = 32
B_PAIR_W = 256
ROPE_BASE = 10000.0

XA_HEADS = 4
XA_HEAD_DIM = 128
FFN_HIDDEN = 2816
FFN_CHUNK = 256

LN_EPS = 1e-5
RMS_EPS = 1e-6
ALPHA = (2 * DEPTH) ** 0.25
LOG2E = math.log2(math.e)

QKV_W = 3 * BRANCH_W
LAT_CKV = B_Q_LORA
LAT_KR = B_Q_LORA + B_KV_LORA
LAT_W = LAT_KR + PAIR_W
CF_ROW0 = 2 * B_ROPE
REL_ROW = 1024

ROW_TILE = 1024
SUB_ROWS = 512

VMEM_LIMIT = 56 * 1024 * 1024

bf16 = jnp.bfloat16
f32 = jnp.float32


def _dot(a, b):
    return jnp.dot(a, b, preferred_element_type=f32)


def _dot_t(a, b):
    return lax.dot_general(a, b, (((1,), (1,)), ((), ())), preferred_element_type=f32)


def _layer_norm(z, g, b):
    mu = jnp.mean(z, axis=-1, keepdims=True)
    zc = z - mu
    var = jnp.mean(zc * zc, axis=-1, keepdims=True)
    return zc * lax.rsqrt(var + LN_EPS) * g + b


def _rms_norm(z, g):
    ms = jnp.mean(z * z, axis=-1, keepdims=True)
    return z * lax.rsqrt(ms + RMS_EPS) * g


def _params(*sem):
    return pltpu.CompilerParams(dimension_semantics=sem, vmem_limit_bytes=VMEM_LIMIT)


def _resident(shape):
    zeros = (0,) * len(shape)
    return pl.BlockSpec(shape, lambda *_: zeros, pipeline_mode=pl.Buffered(1))


def _layer(shape, l):
    zeros = (0,) * len(shape)
    return pl.BlockSpec((None,) + tuple(shape), lambda *_: (l,) + zeros, pipeline_mode=pl.Buffered(1))


def _inproj_kernel(x_ref, wa_ref, wc_ref, wl_ref, cos_ref, sin_ref, gq_ref, gkv_ref,
                   wq_ref, wk_ref, wv_ref,
                   qkva_ref, qkvc_ref, cft_ref, qb_ref, kb_ref, vb_ref):
    qk_scale = HEAD_DIM ** -0.5 * LOG2E
    scale = (B_NOPE + B_ROPE) ** -0.5 * LOG2E
    for r0 in range(0, x_ref.shape[0], SUB_ROWS):
        rows = slice(r0, r0 + SUB_ROWS)
        xb = x_ref[rows, :].astype(bf16)

        for dst, w_ref in ((qkva_ref, wa_ref), (qkvc_ref, wc_ref)):
            q, k, v = (slice(n * BRANCH_W, (n + 1) * BRANCH_W) for n in range(3))
            dst[rows, q] = (_dot_t(xb, w_ref[q, :]) * qk_scale).astype(bf16)
            dst[rows, k] = _dot_t(xb, w_ref[k, :]).astype(bf16)
            dst[rows, v] = _dot_t(xb, w_ref[v, :]).astype(bf16)

        kr_cf = _dot_t(xb, wl_ref[LAT_KR:LAT_W, :])
        cft_ref[:, rows] = kr_cf.T
        cos_t = cos_ref[rows, :]
        sin_t = sin_ref[rows, :]
        cqn = _rms_norm(_dot_t(xb, wl_ref[0:LAT_CKV, :]), gq_ref[...]).astype(bf16)
        q1 = _dot(cqn, wq_ref[...])
        ckvn = _rms_norm(_dot_t(xb, wl_ref[LAT_CKV:LAT_KR, :]), gkv_ref[...]).astype(bf16)
        kn = _dot(ckvn, wk_ref[...])
        vb_ref[rows, :] = _dot(ckvn, wv_ref[...]).astype(bf16)
        lane = lax.broadcasted_iota(jnp.int32, (1, PAIR_W), 1)
        kpe = jnp.where(lane < B_ROPE,
                        kr_cf * cos_t + pltpu.roll(kr_cf, PAIR_W - B_ROPE, 1) * sin_t,
                        pltpu.roll(kr_cf, B_ROPE, 1) * cos_t + kr_cf * sin_t).astype(bf16)
        for p in range(PAIRS):
            c0 = p * B_PAIR_W
            nope, rope_part = slice(c0, c0 + PAIR_W), slice(c0 + PAIR_W, c0 + B_PAIR_W)
            qb_ref[rows, nope] = (q1[:, nope] * scale).astype(bf16)
            blk = q1[:, rope_part]
            pe = blk * cos_t + pltpu.roll(blk, 2 * B_ROPE, 1) * sin_t
            qb_ref[rows, rope_part] = (pe * scale).astype(bf16)
            kb_ref[rows, nope] = kn[:, p * PAIR_W:(p + 1) * PAIR_W].astype(bf16)
            kb_ref[rows, rope_part] = kpe


def _inproj(x, wa, wc, wl, cos_t, sin_t, gq, gkv, wq, wk, wv, l):
    tm = ROW_TILE
    row = lambda i: (i, 0)
    return pl.pallas_call(
        _inproj_kernel,
        grid=(TOKENS // tm,),
        in_specs=[
            pl.BlockSpec((tm, D_MODEL), row),
            _layer((QKV_W, D_MODEL), l),
            _layer((QKV_W, D_MODEL), l),
            _layer((LAT_W, D_MODEL), l),
            pl.BlockSpec((tm, PAIR_W), row),
            pl.BlockSpec((tm, PAIR_W), row),
            _layer((1, B_Q_LORA), l),
            _layer((1, B_KV_LORA), l),
            _layer((B_Q_LORA, PAIRS * B_PAIR_W), l),
            _layer((B_KV_LORA, BRANCH_W), l),
            _layer((B_KV_LORA, BRANCH_W), l),
        ],
        out_specs=[
            pl.BlockSpec((tm, QKV_W), row),
            pl.BlockSpec((tm, QKV_W), row),
            pl.BlockSpec((PAIR_W, tm), lambda i: (0, i)),
            pl.BlockSpec((tm, PAIRS * B_PAIR_W), row),
            pl.BlockSpec((tm, PAIRS * B_PAIR_W), row),
            pl.BlockSpec((tm, BRANCH_W), row),
        ],
        out_shape=[
            jax.ShapeDtypeStruct((TOKENS, QKV_W), bf16),
            jax.ShapeDtypeStruct((TOKENS, QKV_W), bf16),
            jax.ShapeDtypeStruct((PAIR_W, TOKENS), f32),
            jax.ShapeDtypeStruct((TOKENS, PAIRS * B_PAIR_W), bf16),
            jax.ShapeDtypeStruct((TOKENS, PAIRS * B_PAIR_W), bf16),
            jax.ShapeDtypeStruct((TOKENS, BRANCH_W), bf16),
        ],
        compiler_params=_params("parallel"),
        name="inproj",
    )(x, wa, wc, wl, cos_t, sin_t, gq, gkv, wq, wk, wv)


def _rope_kernel(pos_ref, freq_ref, sign_ref, cos_ref, sin_ref):
    ang = pos_ref[...].astype(f32) * freq_ref[...]
    live = (freq_ref[...] > 0.0).astype(f32)
    cos_ref[...] = jnp.cos(ang) * live
    sin_ref[...] = jnp.sin(ang) * sign_ref[...]


def _rope_tables(pos, freq, sign):
    tm = 2048
    row = lambda i: (i, 0)
    return pl.pallas_call(
        _rope_kernel,
        grid=(TOKENS // tm,),
        in_specs=[pl.BlockSpec((tm, 1), row), _resident((1, PAIR_W)), _resident((1, PAIR_W))],
        out_specs=[pl.BlockSpec((tm, PAIR_W), row), pl.BlockSpec((tm, PAIR_W), row)],
        out_shape=[jax.ShapeDtypeStruct((TOKENS, PAIR_W), f32), jax.ShapeDtypeStruct((TOKENS, PAIR_W), f32)],
        compiler_params=_params("parallel"),
        name="rope_tables",
    )(pos, freq, sign)


def _forget_kernel(cf_ref, bf_ref, o_ref):
    rows = BATCH * HEADS
    lane = lax.broadcasted_iota(jnp.int32, (rows, 128), 1)
    bias = jnp.concatenate([bf_ref[...]] * BATCH, axis=0)
    carry = None
    for blk in range(SEQ // 128):
        z = jnp.concatenate([cf_ref[:, b * SEQ + blk * 128:b * SEQ + (blk + 1) * 128]
                             for b in range(BATCH)], axis=0) + bias
        acc = jnp.minimum(z, 0.0) - jnp.log(1.0 + jnp.exp(-jnp.abs(z)))
        d = 1
        while d < 128:
            acc = acc + jnp.where(lane >= d, pltpu.roll(acc, d, 1), 0.0)
            d *= 2
        if blk > 0:
            acc = acc + carry
        for b in range(BATCH):
            o_ref[:, b * SEQ + blk * 128:b * SEQ + (blk + 1) * 128] = acc[b * HEADS:(b + 1) * HEADS]
        carry = acc[:, 127:128]


def _forget_cumsum(cf_t, bf, l):
    return pl.pallas_call(
        _forget_kernel,
        grid=(1,),
        in_specs=[
            pl.BlockSpec((HEADS, TOKENS), lambda i: (CF_ROW0 // HEADS, 0)),
            _layer((HEADS, 1), l),
        ],
        out_specs=pl.BlockSpec((HEADS, TOKENS), lambda i: (0, 0)),
        out_shape=jax.ShapeDtypeStruct((HEADS, TOKENS), f32),
        compiler_params=_params("arbitrary"),
        name="forget_cumsum",
    )(cf_t, bf)


def _rel_table_kernel(w_ref, o_ref):
    x = jnp.broadcast_to(w_ref[...], (ATT_T, REL_ROW))
    x = pltpu.roll(x, REL_ROW - ATT_T, 1, stride=1, stride_axis=0)
    r = lax.broadcasted_iota(jnp.int32, (ATT_T, A_WINDOW), 0)
    c = lax.broadcasted_iota(jnp.int32, (ATT_T, A_WINDOW), 1)
    dchunk = (c >> CHUNK_SHIFT) - (r >> CHUNK_SHIFT)
    valid = (dchunk >= 0) & (dchunk <= A_LEFT_CHUNKS)
    o_ref[...] = jnp.where(valid, x[:, 0:A_WINDOW] * LOG2E, -jnp.inf)


def _rel_table(w_rows, l):
    return pl.pallas_call(
        _rel_table_kernel,
        grid=(HEADS,),
        in_specs=[pl.BlockSpec((None, None, 1, REL_ROW), lambda h: (l, h, 0, 0))],
        out_specs=pl.BlockSpec((None, ATT_T, A_WINDOW), lambda h: (h, 0, 0)),
        out_shape=jax.ShapeDtypeStruct((HEADS, ATT_T, A_WINDOW), f32),
        name="rel_table",
    )(w_rows)


def _pair_select(x, width):
    lane = lax.broadcasted_iota(jnp.int32, (1, width), 1)
    if width == PAIR_W:
        in_a = lane < HEAD_DIM
        in_b = lane >= HEAD_DIM
    else:
        in_a = (lane < 64) | ((lane >= 128) & (lane < 160))
        in_b = ((lane >= 64) & (lane < 128)) | ((lane >= 160) & (lane < 192))
    zero = jnp.zeros_like(x)
    return jnp.where(in_a, x, zero), jnp.where(in_b, x, zero)


def _attn_kernel(*refs, mode, dq, tq, pps):
    if mode == "mla":
        q_ref, k_ref, v_ref, o_ref = refs
        extra_ref = None
    else:
        q_ref, k_ref, v_ref, extra_ref, o_ref = refs
    i = pl.program_id(2)
    nq = SEQ // tq
    neg = jnp.float32(-jnp.inf)
    stack_pv = 2 * tq <= 512

    def attend(k0, nkeys, q0, qrow=0):
        lane = lax.broadcasted_iota(jnp.int32, (1, PAIR_W), 1)
        qrows = slice(qrow, qrow + tq)
        for p in range(pps):
            qa, qb = _pair_select(q_ref[qrows, p * dq:(p + 1) * dq], dq)
            kt = k_ref[pl.ds(k0, nkeys), p * dq:(p + 1) * dq]
            vt = v_ref[pl.ds(k0, nkeys), p * PAIR_W:(p + 1) * PAIR_W]
            s2 = _dot_t(jnp.concatenate([qa, qb], axis=0), kt)
            probs = []
            for head in range(2):
                s = s2[head * tq:(head + 1) * tq]
                if mode == "rel":
                    s = s + extra_ref[2 * p + head, :, A_WINDOW - nkeys:A_WINDOW]
                elif mode == "fox":
                    f = extra_ref[2 * p + head:2 * p + head + 1, :]
                    s = s + (f[:, q0:q0 + 1] - f[:, 0:nkeys]) * LOG2E
                if mode != "rel":
                    row = lax.broadcasted_iota(jnp.int32, (tq, tq), 0)
                    col = lax.broadcasted_iota(jnp.int32, (tq, tq), 1)
                    keep = (col <= row) if mode == "fox" else ((col >> CHUNK_SHIFT) <= (row >> CHUNK_SHIFT))
                    tail = jnp.where(keep, s[:, nkeys - tq:], neg)
                    s = tail if nkeys == tq else jnp.concatenate([s[:, :nkeys - tq], tail], axis=1)
                m = jnp.max(s, axis=-1, keepdims=True)
                probs.append(jnp.exp2(s - m).astype(bf16))
            v_ones = jnp.concatenate([vt, jnp.ones_like(vt)], axis=1)
            if stack_pv:
                o2 = _dot(jnp.concatenate(probs, axis=0), v_ones)
                oa, ob = o2[0:tq], o2[tq:]
            else:
                oa, ob = _dot(probs[0], v_ones), _dot(probs[1], v_ones)
            oa = oa[:, 0:PAIR_W] / oa[:, PAIR_W:]
            ob = ob[:, 0:PAIR_W] / ob[:, PAIR_W:]
            o_ref[qrows, p * PAIR_W:(p + 1) * PAIR_W] = jnp.where(lane < HEAD_DIM, oa, ob).astype(bf16)

    if mode == "rel":
        @pl.when(i == 0)
        def _():
            attend(0, tq, 0, 0)
            attend(0, 2 * tq, 0, tq)

        @pl.when(i >= 1)
        def _():
            attend(pl.multiple_of((2 * i - 2) * tq, tq), A_WINDOW, 0, 0)
            attend(pl.multiple_of((2 * i - 1) * tq, tq), A_WINDOW, 0, tq)
    else:
        for ii in range(nq):
            pl.when(i == ii)(functools.partial(attend, 0, (ii + 1) * tq, ii * tq))


def _attention(q, k, v, extra, *, mode, dq, tq, pps, kcol0, vcol0):
    qtiles = 2 if mode == "rel" else 1
    nq = SEQ // (tq * qtiles)
    groups = PAIRS // pps
    in_specs = [
        pl.BlockSpec((tq * qtiles, pps * dq), lambda b, j, i: (b * nq + i, j)),
        pl.BlockSpec((SEQ, pps * dq), lambda b, j, i: (b, kcol0 // pps + j)),
        pl.BlockSpec((SEQ, pps * PAIR_W), lambda b, j, i: (b, vcol0 // pps + j)),
    ]
    args = [q, k, v]
    if mode == "fox":
        in_specs.append(pl.BlockSpec((None, 2 * pps, SEQ), lambda b, j, i: (j, 0, b)))
        args.append(extra.reshape(groups, 2 * pps, TOKENS))
    elif mode == "rel":
        assert tq == ATT_T
        in_specs.append(pl.BlockSpec((2 * pps, ATT_T, A_WINDOW), lambda b, j, i: (j, 0, 0)))
        args.append(extra)
    return pl.pallas_call(
        functools.partial(_attn_kernel, mode=mode, dq=dq, tq=tq, pps=pps),
        grid=(BATCH, groups, nq),
        in_specs=in_specs,
        out_specs=pl.BlockSpec((tq * qtiles, pps * PAIR_W), lambda b, j, i: (b * nq + i, j)),
        out_shape=jax.ShapeDtypeStruct((TOKENS, BRANCH_W), bf16),
        compiler_params=_params("parallel", "parallel", "arbitrary"),
        name="attn_" + mode,
    )(*args)


def _merge_kernel(x_ref, ya_ref, yb_ref, yc_ref, wg_ref, bg_ref, wb_ref, wo_ref, g_ref, b_ref, o_ref):
    for r0 in range(0, x_ref.shape[0], SUB_ROWS):
        rows = slice(r0, r0 + SUB_ROWS)
        x = x_ref[rows, :]
        xb = x.astype(bf16)
        merged = None
        for n, y_ref in enumerate((ya_ref, yb_ref, yc_ref)):
            cols = slice(n * D_MODEL, (n + 1) * D_MODEL)
            z = _dot_t(xb, wg_ref[cols, :]) + bg_ref[:, cols]
            term = _dot(y_ref[rows, :], wb_ref[n]) / (1.0 + jnp.exp(-z))
            merged = term if merged is None else merged + term
        y = _dot(merged.astype(bf16), wo_ref[...])
        o_ref[rows, :] = _layer_norm(ALPHA * x + y, g_ref[...], b_ref[...])


def _merge(x, ya, yb, yc, wg, bg, wb, wo, g, b, l):
    tm = ROW_TILE
    row = lambda i: (i, 0)
    return pl.pallas_call(
        _merge_kernel,
        grid=(TOKENS // tm,),
        in_specs=[
            pl.BlockSpec((tm, D_MODEL), row),
            pl.BlockSpec((tm, BRANCH_W), row),
            pl.BlockSpec((tm, BRANCH_W), row),
            pl.BlockSpec((tm, BRANCH_W), row),
            _layer((3 * D_MODEL, D_MODEL), l),
            _layer((1, 3 * D_MODEL), l),
            _layer((3, BRANCH_W, D_MODEL), l),
            _layer((D_MODEL, D_MODEL), l),
            _layer((1, D_MODEL), l),
            _layer((1, D_MODEL), l),
        ],
        out_specs=pl.BlockSpec((tm, D_MODEL), row),
        out_shape=jax.ShapeDtypeStruct((TOKENS, D_MODEL), f32),
        compiler_params=_params("parallel"),
        name="merge",
    )(x, ya, yb, yc, wg, bg, wb, wo, g, b)


def _xattn_kernel(x_ref, mem_ref, wq_ref, wkv_ref, wo_ref, g_ref, b_ref, o_ref, kv_ref):
    @pl.when(pl.program_id(1) == 0)
    def _():
        kv_ref[...] = _dot(mem_ref[...].astype(bf16), wkv_ref[...]).astype(bf16)

    scale = XA_HEAD_DIM ** -0.5 * LOG2E
    width = XA_HEADS * XA_HEAD_DIM
    for r0 in range(0, x_ref.shape[0], SUB_ROWS):
        rows = slice(r0, r0 + SUB_ROWS)
        x = x_ref[rows, :]
        q = (_dot(x.astype(bf16), wq_ref[...]) * scale).astype(bf16)
        outs = []
        for h in range(XA_HEADS):
            c0 = h * XA_HEAD_DIM
            s = _dot_t(q[:, c0:c0 + XA_HEAD_DIM], kv_ref[:, c0:c0 + XA_HEAD_DIM])
            p = jnp.exp2(s - jnp.max(s, axis=-1, keepdims=True))
            l = jnp.sum(p, axis=-1, keepdims=True)
            o = _dot(p.astype(bf16), kv_ref[:, width + c0:width + c0 + XA_HEAD_DIM])
            outs.append((o / l).astype(bf16))
        y = _dot(jnp.concatenate(outs, axis=-1), wo_ref[...])
        o_ref[rows, :] = _layer_norm(ALPHA * x + y, g_ref[...], b_ref[...])


def _xattn(x, mem, wq, wkv, wo, g, b, l):
    tm = ROW_TILE
    nq = SEQ // tm
    width = XA_HEADS * XA_HEAD_DIM
    return pl.pallas_call(
        _xattn_kernel,
        grid=(BATCH, nq),
        in_specs=[
            pl.BlockSpec((tm, D_MODEL), lambda bi, i: (bi * nq + i, 0)),
            pl.BlockSpec((MEM_LEN, D_MODEL), lambda bi, i: (bi, 0)),
            _layer((D_MODEL, width), l),
            _layer((D_MODEL, 2 * width), l),
            _layer((width, D_MODEL), l),
            _layer((1, D_MODEL), l),
            _layer((1, D_MODEL), l),
        ],
        out_specs=pl.BlockSpec((tm, D_MODEL), lambda bi, i: (bi * nq + i, 0)),
        out_shape=jax.ShapeDtypeStruct((TOKENS, D_MODEL), f32),
        scratch_shapes=[pltpu.VMEM((MEM_LEN, 2 * width), bf16)],
        compiler_params=_params("arbitrary", "arbitrary"),
        name="xattn",
    )(x, mem, wq, wkv, wo, g, b)


def _ffn_kernel(x_ref, wgu_ref, wd_ref, g_ref, b_ref, o_ref, acc_ref):
    for r0 in range(0, x_ref.shape[0], SUB_ROWS):
        rows = slice(r0, r0 + SUB_ROWS)
        x = x_ref[rows, :]
        xb = x.astype(bf16)
        for c in range(FFN_HIDDEN // FFN_CHUNK):
            c0 = c * FFN_CHUNK
            gate = _dot(xb, wgu_ref[:, c0:c0 + FFN_CHUNK])
            up = _dot(xb, wgu_ref[:, FFN_HIDDEN + c0:FFN_HIDDEN + c0 + FFN_CHUNK])
            h = (gate / (1.0 + jnp.exp(-gate)) * up).astype(bf16)
            part = _dot(h, wd_ref[c0:c0 + FFN_CHUNK, :])
            if c == 0:
                acc_ref[rows, :] = part
            else:
                acc_ref[rows, :] += part
        o_ref[rows, :] = _layer_norm(ALPHA * x + acc_ref[rows, :], g_ref[...], b_ref[...])


def _ffn(x, wgu, wd, g, b, l):
    tm = ROW_TILE
    row = lambda i: (i, 0)
    return pl.pallas_call(
        _ffn_kernel,
        grid=(TOKENS // tm,),
        in_specs=[
            pl.BlockSpec((tm, D_MODEL), row),
            _layer((D_MODEL, 2 * FFN_HIDDEN), l),
            _layer((FFN_HIDDEN, D_MODEL), l),
            _layer((1, D_MODEL), l),
            _layer((1, D_MODEL), l),
        ],
        out_specs=pl.BlockSpec((tm, D_MODEL), row),
        out_shape=jax.ShapeDtypeStruct((TOKENS, D_MODEL), f32),
        scratch_shapes=[pltpu.VMEM((tm, D_MODEL), f32)],
        compiler_params=_params("parallel"),
        name="ffn",
    )(x, wgu, wd, g, b)


def _arrange_w_in(w):
    a0, b0 = 0, QKV_W
    c0 = b0 + B_Q_LORA + B_KV_LORA + B_ROPE
    f0 = c0 + QKV_W
    g0 = f0 + HEADS
    wt = jnp.swapaxes(w, 1, 2)
    a = wt[:, a0:b0].astype(bf16)
    c = wt[:, c0:f0].astype(bf16)
    g = wt[:, g0:g0 + 3 * D_MODEL].astype(bf16)
    small = wt[:, b0:c0].astype(bf16)
    cq_ckv = small[:, 0:LAT_KR]
    kr = small[:, LAT_KR:]
    cf = wt[:, f0:g0].astype(bf16)
    half = B_ROPE // 2
    kr_sw = jnp.concatenate([kr[:, half:], kr[:, :half]], axis=1)
    pad = jnp.zeros((w.shape[0], PAIR_W - 2 * B_ROPE - HEADS, w.shape[1]), bf16)
    lat = jnp.concatenate([cq_ckv, kr, kr_sw, cf, pad], axis=1)
    return a, c, g, lat


def _arrange_w_uq(w):
    lead = w.shape[:2]
    w = w.reshape(lead + (PAIRS, 2, B_NOPE + B_ROPE))
    nope = w[..., :B_NOPE].reshape(lead + (PAIRS, 2 * B_NOPE))
    pe = w[..., B_NOPE:]
    half = B_ROPE // 2
    pe_sw = jnp.concatenate([pe[..., half:], pe[..., :half]], axis=-1)
    pe = pe.reshape(lead + (PAIRS, 2 * B_ROPE))
    pe_sw = pe_sw.reshape(lead + (PAIRS, 2 * B_ROPE))
    return jnp.concatenate([nope, pe, pe_sw], axis=-1).reshape(lead + (PAIRS * B_PAIR_W,)).astype(bf16)


def _arrange_w_ukv(w):
    lead = w.shape[:2]
    w = w.reshape(lead + (HEADS, B_NOPE + HEAD_DIM))
    wk = w[..., :B_NOPE].reshape(lead + (BRANCH_W,))
    wv = w[..., B_NOPE:].reshape(lead + (BRANCH_W,))
    return wk.astype(bf16), wv.astype(bf16)


def _rel_bias_rows(rel_bias):
    lead = rel_bias.shape[:2]
    n_hi = A_WINDOW - ATT_T
    lo = jnp.broadcast_to(rel_bias[..., :1], lead + (REL_ROW - n_hi - rel_bias.shape[-1],))
    hi = jnp.broadcast_to(rel_bias[..., -1:], lead + (n_hi,))
    rows = jnp.concatenate([hi, rel_bias[..., ::-1], lo], axis=-1).astype(f32)
    return rows.reshape(lead + (1, REL_ROW))


def kernel(x, mem, positions, ln_mix_g, ln_mix_b, w_in, b_gate, b_forget, a_rel_bias, b_q_norm, b_kv_norm, b_w_uq, b_w_ukv, w_branch, w_mix_out, ln_xa_g, ln_xa_b, xa_w_q, xa_w_kv, xa_w_o, ln_ffn_g, ln_ffn_b, ffn_w_gu, ffn_w_down):
    xf = x.reshape(TOKENS, D_MODEL)
    memf = mem.reshape(BATCH * MEM_LEN, D_MODEL)
    pos = positions.reshape(TOKENS, 1)

    half = B_ROPE // 2
    inv_freq = ROPE_BASE ** (-jnp.arange(half, dtype=f32) / half)
    dead = jnp.zeros((PAIR_W - 2 * B_ROPE,), f32)
    freq_row = jnp.concatenate([jnp.tile(inv_freq, 4), dead]).reshape(1, PAIR_W)
    sign_row = jnp.concatenate([jnp.tile(jnp.concatenate([-jnp.ones((half,), f32), jnp.ones((half,), f32)]), 2),
                                dead]).reshape(1, PAIR_W)
    cos_t, sin_t = _rope_tables(pos, freq_row, sign_row)

    vec = lambda p: p.reshape(DEPTH, 1, p.shape[-1])
    wa, wc, wg, wl = _arrange_w_in(w_in)
    wq = _arrange_w_uq(b_w_uq)
    wk, wv = _arrange_w_ukv(b_w_ukv)
    rel_rows = _rel_bias_rows(a_rel_bias)
    w_branch_b, w_out_b = w_branch.astype(bf16), w_mix_out.astype(bf16)
    xa_q, xa_kv, xa_o = xa_w_q.astype(bf16), xa_w_kv.astype(bf16), xa_w_o.astype(bf16)
    w_gu, w_down = ffn_w_gu.astype(bf16), ffn_w_down.astype(bf16)
    bg = b_gate.reshape(DEPTH, 1, 3 * D_MODEL)
    bf = b_forget.reshape(DEPTH, HEADS, 1)

    for l in range(DEPTH):
        qkv_a, qkv_c, cf_t, q_b, k_b, v_b = _inproj(
            xf, wa, wc, wl, cos_t, sin_t, vec(b_q_norm), vec(b_kv_norm), wq, wk, wv, l)
        forget = _forget_cumsum(cf_t, bf, l)
        rel_table = _rel_table(rel_rows, l)

        y_a = _attention(qkv_a, qkv_a, qkv_a, rel_table, mode="rel", dq=PAIR_W, tq=ATT_T, pps=4,
                         kcol0=PAIRS, vcol0=2 * PAIRS)
        y_b = _attention(q_b, k_b, v_b, None, mode="mla", dq=B_PAIR_W, tq=CAUSAL_T, pps=2, kcol0=0, vcol0=0)
        y_c = _attention(qkv_c, qkv_c, qkv_c, forget, mode="fox", dq=PAIR_W, tq=CAUSAL_T, pps=2,
                         kcol0=PAIRS, vcol0=2 * PAIRS)

        xf = _merge(xf, y_a, y_b, y_c, wg, bg, w_branch_b, w_out_b, vec(ln_mix_g), vec(ln_mix_b), l)
        xf = _xattn(xf, memf, xa_q, xa_kv, xa_o, vec(ln_xa_g), vec(ln_xa_b), l)
        xf = _ffn(xf, w_gu, w_down, vec(ln_ffn_g), vec(ln_ffn_b), l)
    return xf.reshape(BATCH, SEQ, D_MODEL)
```

```python
import functools
import math

import jax
import jax.numpy as jnp
from jax import lax
from jax.experimental import pallas as pl
from jax.experimental.pallas import tpu as pltpu

D_MODEL = 1024
BATCH = 8
SEQ = 2048
DEPTH = 2
TOKENS = BATCH * SEQ
CHUNK = 64
CHUNK_SHIFT = CHUNK.bit_length() - 1
MEM_LEN = 256

HEADS = 8
HEAD_DIM = 64
PAIRS = HEADS // 2
PAIR_W = 2 * HEAD_DIM
BRANCH_W = HEADS * HEAD_DIM

A_LEFT_CHUNKS = 8
ATT_T = 256
A_WINDOW = 3 * ATT_T
CAUSAL_T = 512

B_Q_LORA = 384
B_KV_LORA = 256
B_NOPE = 64
B_ROPE = 32
B_PAIR_W = 256
ROPE_BASE = 10000.0

XA_HEADS = 4
XA_HEAD_DIM = 128
FFN_HIDDEN = 2816
FFN_CHUNK = 256

LN_EPS = 1e-5
RMS_EPS = 1e-6
ALPHA = (2 * DEPTH) ** 0.25
LOG2E = math.log2(math.e)

QKV_W = 3 * BRANCH_W
LAT_CKV = B_Q_LORA
LAT_KR = B_Q_LORA + B_KV_LORA
LAT_W = LAT_KR + PAIR_W
CF_ROW0 = 2 * B_ROPE
REL_ROW = 1024

ROW_TILE = 1024
SUB_ROWS = 512

VMEM_LIMIT = 56 * 1024 * 1024

bf16 = jnp.bfloat16
f32 = jnp.float32


def _dot(a, b):
    return jnp.dot(a, b, preferred_element_type=f32)


def _dot_t(a, b):
    return lax.dot_general(a, b, (((1,), (1,)), ((), ())), preferred_element_type=f32)


def _layer_norm(z, g, b):
    mu = jnp.mean(z, axis=-1, keepdims=True)
    zc = z - mu
    var = jnp.mean(zc * zc, axis=-1, keepdims=True)
    return zc * lax.rsqrt(var + LN_EPS) * g + b


def _rms_norm(z, g):
    ms = jnp.mean(z * z, axis=-1, keepdims=True)
    return z * lax.rsqrt(ms + RMS_EPS) * g


def _params(*sem):
    return pltpu.CompilerParams(dimension_semantics=sem, vmem_limit_bytes=VMEM_LIMIT)


def _resident(shape):
    zeros = (0,) * len(shape)
    return pl.BlockSpec(shape, lambda *_: zeros, pipeline_mode=pl.Buffered(1))


def _layer(shape, l):
    zeros = (0,) * len(shape)
    return pl.BlockSpec((None,) + tuple(shape), lambda *_: (l,) + zeros, pipeline_mode=pl.Buffered(1))


def _inproj_kernel(x_ref, wa_ref, wc_ref, wl_ref, cos_ref, sin_ref, gq_ref, gkv_ref,
                   wq_ref, wk_ref, wv_ref,
                   qkva_ref, qkvc_ref, cft_ref, qb_ref, kb_ref, vb_ref):
    qk_scale = HEAD_DIM ** -0.5 * LOG2E
    scale = (B_NOPE + B_ROPE) ** -0.5 * LOG2E
    for r0 in range(0, x_ref.shape[0], SUB_ROWS):
        rows = slice(r0, r0 + SUB_ROWS)
        xb = x_ref[rows, :].astype(bf16)

        for dst, w_ref in ((qkva_ref, wa_ref), (qkvc_ref, wc_ref)):
            q, k, v = (slice(n * BRANCH_W, (n + 1) * BRANCH_W) for n in range(3))
            dst[rows, q] = (_dot_t(xb, w_ref[q, :]) * qk_scale).astype(bf16)
            dst[rows, k] = _dot_t(xb, w_ref[k, :]).astype(bf16)
            dst[rows, v] = _dot_t(xb, w_ref[v, :]).astype(bf16)

        kr_cf = _dot_t(xb, wl_ref[LAT_KR:LAT_W, :])
        cft_ref[:, rows] = kr_cf.T
        cos_t = cos_ref[rows, :]
        sin_t = sin_ref[rows, :]
        cqn = _rms_norm(_dot_t(xb, wl_ref[0:LAT_CKV, :]), gq_ref[...]).astype(bf16)
        q1 = _dot(cqn, wq_ref[...])
        ckvn = _rms_norm(_dot_t(xb, wl_ref[LAT_CKV:LAT_KR, :]), gkv_ref[...]).astype(bf16)
        kn = _dot(ckvn, wk_ref[...])
        vb_ref[rows, :] = _dot(ckvn, wv_ref[...]).astype(bf16)
        lane = lax.broadcasted_iota(jnp.int32, (1, PAIR_W), 1)
        kpe = jnp.where(lane < B_ROPE,
                        kr_cf * cos_t + pltpu.roll(kr_cf, PAIR_W - B_ROPE, 1) * sin_t,
                        pltpu.roll(kr_cf, B_ROPE, 1) * cos_t + kr_cf * sin_t).astype(bf16)
        for p in range(PAIRS):
            c0 = p * B_PAIR_W
            nope, rope_part = slice(c0, c0 + PAIR_W), slice(c0 + PAIR_W, c0 + B_PAIR_W)
            qb_ref[rows, nope] = (q1[:, nope] * scale).astype(bf16)
            blk = q1[:, rope_part]
            pe = blk * cos_t + pltpu.roll(blk, 2 * B_ROPE, 1) * sin_t
            qb_ref[rows, rope_part] = (pe * scale).astype(bf16)
            kb_ref[rows, nope] = kn[:, p * PAIR_W:(p + 1) * PAIR_W].astype(bf16)
            kb_ref[rows, rope_part] = kpe


def _inproj(x, wa, wc, wl, cos_t, sin_t, gq, gkv, wq, wk, wv, l):
    tm = ROW_TILE
    row = lambda i: (i, 0)
    return pl.pallas_call(
        _inproj_kernel,
        grid=(TOKENS // tm,),
        in_specs=[
            pl.BlockSpec((tm, D_MODEL), row),
            _layer((QKV_W, D_MODEL), l),
            _layer((QKV_W, D_MODEL), l),
            _layer((LAT_W, D_MODEL), l),
            pl.BlockSpec((tm, PAIR_W), row),
            pl.BlockSpec((tm, PAIR_W), row),
            _layer((1, B_Q_LORA), l),
            _layer((1, B_KV_LORA), l),
            _layer((B_Q_LORA, PAIRS * B_PAIR_W), l),
            _layer((B_KV_LORA, BRANCH_W), l),
            _layer((B_KV_LORA, BRANCH_W), l),
        ],
        out_specs=[
            pl.BlockSpec((tm, QKV_W), row),
            pl.BlockSpec((tm, QKV_W), row),
            pl.BlockSpec((PAIR_W, tm), lambda i: (0, i)),
            pl.BlockSpec((tm, PAIRS * B_PAIR_W), row),
            pl.BlockSpec((tm, PAIRS * B_PAIR_W), row),
            pl.BlockSpec((tm, BRANCH_W), row),
        ],
        out_shape=[
            jax.ShapeDtypeStruct((TOKENS, QKV_W), bf16),
            jax.ShapeDtypeStruct((TOKENS, QKV_W), bf16),
            jax.ShapeDtypeStruct((PAIR_W, TOKENS), f32),
            jax.ShapeDtypeStruct((TOKENS, PAIRS * B_PAIR_W), bf16),
            jax.ShapeDtypeStruct((TOKENS, PAIRS * B_PAIR_W), bf16),
            jax.ShapeDtypeStruct((TOKENS, BRANCH_W), bf16),
        ],
        compiler_params=_params("parallel"),
        name="inproj",
    )(x, wa, wc, wl, cos_t, sin_t, gq, gkv, wq, wk, wv)


def _rope_kernel(pos_ref, freq_ref, sign_ref, cos_ref, sin_ref):
    ang = pos_ref[...].astype(f32) * freq_ref[...]
    live = (freq_ref[...] > 0.0).astype(f32)
    cos_ref[...] = jnp.cos(ang) * live
    sin_ref[...] = jnp.sin(ang) * sign_ref[...]


def _rope_tables(pos, freq, sign):
    tm = 2048
    row = lambda i: (i, 0)
    return pl.pallas_call(
        _rope_kernel,
        grid=(TOKENS // tm,),
        in_specs=[pl.BlockSpec((tm, 1), row), _resident((1, PAIR_W)), _resident((1, PAIR_W))],
        out_specs=[pl.BlockSpec((tm, PAIR_W), row), pl.BlockSpec((tm, PAIR_W), row)],
        out_shape=[jax.ShapeDtypeStruct((TOKENS, PAIR_W), f32), jax.ShapeDtypeStruct((TOKENS, PAIR_W), f32)],
        compiler_params=_params("parallel"),
        name="rope_tables",
    )(pos, freq, sign)


def _forget_kernel(cf_ref, bf_ref, o_ref):
    rows = BATCH * HEADS
    lane = lax.broadcasted_iota(jnp.int32, (rows, 128), 1)
    bias = jnp.concatenate([bf_ref[...]] * BATCH, axis=0)
    carry = None
    for blk in range(SEQ // 128):
        z = jnp.concatenate([cf_ref[:, b * SEQ + blk * 128:b * SEQ + (blk + 1) * 128]
                             for b in range(BATCH)], axis=0) + bias
        acc = jnp.minimum(z, 0.0) - jnp.log(1.0 + jnp.exp(-jnp.abs(z)))
        d = 1
        while d < 128:
            acc = acc + jnp.where(lane >= d, pltpu.roll(acc, d, 1), 0.0)
            d *= 2
        if blk > 0:
            acc = acc + carry
        for b in range(BATCH):
            o_ref[:, b * SEQ + blk * 128:b * SEQ + (blk + 1) * 128] = acc[b * HEADS:(b + 1) * HEADS]
        carry = acc[:, 127:128]


def _forget_cumsum(cf_t, bf, l):
    return pl.pallas_call(
        _forget_kernel,
        grid=(1,),
        in_specs=[
            pl.BlockSpec((HEADS, TOKENS), lambda i: (CF_ROW0 // HEADS, 0)),
            _layer((HEADS, 1), l),
        ],
        out_specs=pl.BlockSpec((HEADS, TOKENS), lambda i: (0, 0)),
        out_shape=jax.ShapeDtypeStruct((HEADS, TOKENS), f32),
        compiler_params=_params("arbitrary"),
        name="forget_cumsum",
    )(cf_t, bf)


def _rel_table_kernel(w_ref, o_ref):
    x = jnp.broadcast_to(w_ref[...], (ATT_T, REL_ROW))
    x = pltpu.roll(x, REL_ROW - ATT_T, 1, stride=1, stride_axis=0)
    r = lax.broadcasted_iota(jnp.int32, (ATT_T, A_WINDOW), 0)
    c = lax.broadcasted_iota(jnp.int32, (ATT_T, A_WINDOW), 1)
    dchunk = (c >> CHUNK_SHIFT) - (r >> CHUNK_SHIFT)
    valid = (dchunk >= 0) & (dchunk <= A_LEFT_CHUNKS)
    o_ref[...] = jnp.where(valid, x[:, 0:A_WINDOW] * LOG2E, -jnp.inf)


def _rel_table(w_rows, l):
    return pl.pallas_call(
        _rel_table_kernel,
        grid=(HEADS,),
        in_specs=[pl.BlockSpec((None, None, 1, REL_ROW), lambda h: (l, h, 0, 0))],
        out_specs=pl.BlockSpec((None, ATT_T, A_WINDOW), lambda h: (h, 0, 0)),
        out_shape=jax.ShapeDtypeStruct((HEADS, ATT_T, A_WINDOW), f32),
        name="rel_table",
    )(w_rows)


def _pair_select(x, width):
    lane = lax.broadcasted_iota(jnp.int32, (1, width), 1)
    if width == PAIR_W:
        in_a = lane < HEAD_DIM
        in_b = lane >= HEAD_DIM
    else:
        in_a = (lane < 64) | ((lane >= 128) & (lane < 160))
        in_b = ((lane >= 64) & (lane < 128)) | ((lane >= 160) & (lane < 192))
    zero = jnp.zeros_like(x)
    return jnp.where(in_a, x, zero), jnp.where(in_b, x, zero)


def _attn_kernel(*refs, mode, dq, tq, pps):
    if mode == "mla":
        q_ref, k_ref, v_ref, o_ref = refs
        extra_ref = None
    else:
        q_ref, k_ref, v_ref, extra_ref, o_ref = refs
    i = pl.program_id(2)
    nq = SEQ // tq
    neg = jnp.float32(-jnp.inf)
    stack_pv = 2 * tq <= 512

    def attend(k0, nkeys, q0, qrow=0):
        lane = lax.broadcasted_iota(jnp.int32, (1, PAIR_W), 1)
        qrows = slice(qrow, qrow + tq)
        for p in range(pps):
            qa, qb = _pair_select(q_ref[qrows, p * dq:(p + 1) * dq], dq)
            kt = k_ref[pl.ds(k0, nkeys), p * dq:(p + 1) * dq]
            vt = v_ref[pl.ds(k0, nkeys), p * PAIR_W:(p + 1) * PAIR_W]
            s2 = _dot_t(jnp.concatenate([qa, qb], axis=0), kt)
            probs = []
            for head in range(2):
                s = s2[head * tq:(head + 1) * tq]
                if mode == "rel":
                    s = s + extra_ref[2 * p + head, :, A_WINDOW - nkeys:A_WINDOW]
                elif mode == "fox":
                    f = extra_ref[2 * p + head:2 * p + head + 1, :]
                    s = s + (f[:, q0:q0 + 1] - f[:, 0:nkeys]) * LOG2E
                if mode != "rel":
                    row = lax.broadcasted_iota(jnp.int32, (tq, tq), 0)
                    col = lax.broadcasted_iota(jnp.int32, (tq, tq), 1)
                    keep = (col <= row) if mode == "fox" else ((col >> CHUNK_SHIFT) <= (row >> CHUNK_SHIFT))
                    tail = jnp.where(keep, s[:, nkeys - tq:], neg)
                    s = tail if nkeys == tq else jnp.concatenate([s[:, :nkeys - tq], tail], axis=1)
                m = jnp.max(s, axis=-1, keepdims=True)
                probs.append(jnp.exp2(s - m).astype(bf16))
            v_ones = jnp.concatenate([vt, jnp.ones_like(vt)], axis=1)
            if stack_pv:
                o2 = _dot(jnp.concatenate(probs, axis=0), v_ones)
                oa, ob = o2[0:tq], o2[tq:]
            else:
                oa, ob = _dot(probs[0], v_ones), _dot(probs[1], v_ones)
            oa = oa[:, 0:PAIR_W] / oa[:, PAIR_W:]
            ob = ob[:, 0:PAIR_W] / ob[:, PAIR_W:]
            o_ref[qrows, p * PAIR_W:(p + 1) * PAIR_W] = jnp.where(lane < HEAD_DIM, oa, ob).astype(bf16)

    if mode == "rel":
        @pl.when(i == 0)
        def _():
            attend(0, tq, 0, 0)
            attend(0, 2 * tq, 0, tq)

        @pl.when(i >= 1)
        def _():
            attend(pl.multiple_of((2 * i - 2) * tq, tq), A_WINDOW, 0, 0)
            attend(pl.multiple_of((2 * i - 1) * tq, tq), A_WINDOW, 0, tq)
    else:
        def two_tiles(first):
            attend(0, (first + 1) * tq, first * tq, 0)
            attend(0, (first + 2) * tq, (first + 1) * tq, tq)

        for step in range(nq // 2):
            pl.when(i == step)(functools.partial(two_tiles, 2 * step))


def _attention(q, k, v, extra, *, mode, dq, tq, pps, kcol0, vcol0):
    qtiles = 2
    nq = SEQ // (tq * qtiles)
    groups = PAIRS // pps
    in_specs = [
        pl.BlockSpec((tq * qtiles, pps * dq), lambda b, j, i: (b * nq + i, j)),
        pl.BlockSpec((SEQ, pps * dq), lambda b, j, i: (b, kcol0 // pps + j)),
        pl.BlockSpec((SEQ, pps * PAIR_W), lambda b, j, i: (b, vcol0 // pps + j)),
    ]
    args = [q, k, v]
    if mode == "fox":
        in_specs.append(pl.BlockSpec((None, 2 * pps, SEQ), lambda b, j, i: (j, 0, b)))
        args.append(extra.reshape(groups, 2 * pps, TOKENS))
    elif mode == "rel":
        assert tq == ATT_T
        in_specs.append(pl.BlockSpec((2 * pps, ATT_T, A_WINDOW), lambda b, j, i: (j, 0, 0)))
        args.append(extra)
    return pl.pallas_call(
        functools.partial(_attn_kernel, mode=mode, dq=dq, tq=tq, pps=pps),
        grid=(BATCH, groups, nq),
        in_specs=in_specs,
        out_specs=pl.BlockSpec((tq * qtiles, pps * PAIR_W), lambda b, j, i: (b * nq + i, j)),
        out_shape=jax.ShapeDtypeStruct((TOKENS, BRANCH_W), bf16),
        compiler_params=_params("parallel", "parallel", "arbitrary"),
        name="attn_" + mode,
    )(*args)


def _merge_kernel(x_ref, ya_ref, yb_ref, yc_ref, wg_ref, bg_ref, wb_ref, wo_ref, g_ref, b_ref, o_ref):
    for r0 in range(0, x_ref.shape[0], SUB_ROWS):
        rows = slice(r0, r0 + SUB_ROWS)
        x = x_ref[rows, :]
        xb = x.astype(bf16)
        merged = None
        for n, y_ref in enumerate((ya_ref, yb_ref, yc_ref)):
            cols = slice(n * D_MODEL, (n + 1) * D_MODEL)
            z = _dot_t(xb, wg_ref[cols, :]) + bg_ref[:, cols]
            term = _dot(y_ref[rows, :], wb_ref[n]) / (1.0 + jnp.exp(-z))
            merged = term if merged is None else merged + term
        y = _dot(merged.astype(bf16), wo_ref[...])
        o_ref[rows, :] = _layer_norm(ALPHA * x + y, g_ref[...], b_ref[...])


def _merge(x, ya, yb, yc, wg, bg, wb, wo, g, b, l):
    tm = ROW_TILE
    row = lambda i: (i, 0)
    return pl.pallas_call(
        _merge_kernel,
        grid=(TOKENS // tm,),
        in_specs=[
            pl.BlockSpec((tm, D_MODEL), row),
            pl.BlockSpec((tm, BRANCH_W), row),
            pl.BlockSpec((tm, BRANCH_W), row),
            pl.BlockSpec((tm, BRANCH_W), row),
            _layer((3 * D_MODEL, D_MODEL), l),
            _layer((1, 3 * D_MODEL), l),
            _layer((3, BRANCH_W, D_MODEL), l),
            _layer((D_MODEL, D_MODEL), l),
            _layer((1, D_MODEL), l),
            _layer((1, D_MODEL), l),
        ],
        out_specs=pl.BlockSpec((tm, D_MODEL), row),
        out_shape=jax.ShapeDtypeStruct((TOKENS, D_MODEL), f32),
        compiler_params=_params("parallel"),
        name="merge",
    )(x, ya, yb, yc, wg, bg, wb, wo, g, b)


def _xattn_kernel(x_ref, mem_ref, wq_ref, wkv_ref, wo_ref, g_ref, b_ref, o_ref, kv_ref):
    @pl.when(pl.program_id(1) == 0)
    def _():
        kv_ref[...] = _dot(mem_ref[...].astype(bf16), wkv_ref[...]).astype(bf16)

    scale = XA_HEAD_DIM ** -0.5 * LOG2E
    width = XA_HEADS * XA_HEAD_DIM
    for r0 in range(0, x_ref.shape[0], SUB_ROWS):
        rows = slice(r0, r0 + SUB_ROWS)
        x = x_ref[rows, :]
        q = (_dot(x.astype(bf16), wq_ref[...]) * scale).astype(bf16)
        outs = []
        for h in range(XA_HEADS):
            c0 = h * XA_HEAD_DIM
            s = _dot_t(q[:, c0:c0 + XA_HEAD_DIM], kv_ref[:, c0:c0 + XA_HEAD_DIM])
            p = jnp.exp2(s - jnp.max(s, axis=-1, keepdims=True))
            l = jnp.sum(p, axis=-1, keepdims=True)
            o = _dot(p.astype(bf16), kv_ref[:, width + c0:width + c0 + XA_HEAD_DIM])
            outs.append((o / l).astype(bf16))
        y = _dot(jnp.concatenate(outs, axis=-1), wo_ref[...])
        o_ref[rows, :] = _layer_norm(ALPHA * x + y, g_ref[...], b_ref[...])


def _xattn(x, mem, wq, wkv, wo, g, b, l):
    tm = ROW_TILE
    nq = SEQ // tm
    width = XA_HEADS * XA_HEAD_DIM
    return pl.pallas_call(
        _xattn_kernel,
        grid=(BATCH, nq),
        in_specs=[
            pl.BlockSpec((tm, D_MODEL), lambda bi, i: (bi * nq + i, 0)),
            pl.BlockSpec((MEM_LEN, D_MODEL), lambda bi, i: (bi, 0)),
            _layer((D_MODEL, width), l),
            _layer((D_MODEL, 2 * width), l),
            _layer((width, D_MODEL), l),
            _layer((1, D_MODEL), l),
            _layer((1, D_MODEL), l),
        ],
        out_specs=pl.BlockSpec((tm, D_MODEL), lambda bi, i: (bi * nq + i, 0)),
        out_shape=jax.ShapeDtypeStruct((TOKENS, D_MODEL), f32),
        scratch_shapes=[pltpu.VMEM((MEM_LEN, 2 * width), bf16)],
        compiler_params=_params("arbitrary", "arbitrary"),
        name="xattn",
    )(x, mem, wq, wkv, wo, g, b)


def _ffn_kernel(x_ref, wgu_ref, wd_ref, g_ref, b_ref, o_ref, acc_ref):
    for r0 in range(0, x_ref.shape[0], SUB_ROWS):
        rows = slice(r0, r0 + SUB_ROWS)
        x = x_ref[rows, :]
        xb = x.astype(bf16)
        for c in range(FFN_HIDDEN // FFN_CHUNK):
            c0 = c * FFN_CHUNK
            gate = _dot(xb, wgu_ref[:, c0:c0 + FFN_CHUNK])
            up = _dot(xb, wgu_ref[:, FFN_HIDDEN + c0:FFN_HIDDEN + c0 + FFN_CHUNK])
            h = (gate / (1.0 + jnp.exp(-gate)) * up).astype(bf16)
            part = _dot(h, wd_ref[c0:c0 + FFN_CHUNK, :])
            if c == 0:
                acc_ref[rows, :] = part
            else:
                acc_ref[rows, :] += part
        o_ref[rows, :] = _layer_norm(ALPHA * x + acc_ref[rows, :], g_ref[...], b_ref[...])


def _ffn(x, wgu, wd, g, b, l):
    tm = ROW_TILE
    row = lambda i: (i, 0)
    return pl.pallas_call(
        _ffn_kernel,
        grid=(TOKENS // tm,),
        in_specs=[
            pl.BlockSpec((tm, D_MODEL), row),
            _layer((D_MODEL, 2 * FFN_HIDDEN), l),
            _layer((FFN_HIDDEN, D_MODEL), l),
            _layer((1, D_MODEL), l),
            _layer((1, D_MODEL), l),
        ],
        out_specs=pl.BlockSpec((tm, D_MODEL), row),
        out_shape=jax.ShapeDtypeStruct((TOKENS, D_MODEL), f32),
        scratch_shapes=[pltpu.VMEM((tm, D_MODEL), f32)],
        compiler_params=_params("parallel"),
        name="ffn",
    )(x, wgu, wd, g, b)


def _arrange_w_in(w):
    a0, b0 = 0, QKV_W
    c0 = b0 + B_Q_LORA + B_KV_LORA + B_ROPE
    f0 = c0 + QKV_W
    g0 = f0 + HEADS
    wt = jnp.swapaxes(w, 1, 2)
    a = wt[:, a0:b0].astype(bf16)
    c = wt[:, c0:f0].astype(bf16)
    g = wt[:, g0:g0 + 3 * D_MODEL].astype(bf16)
    small = wt[:, b0:c0].astype(bf16)
    cq_ckv = small[:, 0:LAT_KR]
    kr = small[:, LAT_KR:]
    cf = wt[:, f0:g0].astype(bf16)
    half = B_ROPE // 2
    kr_sw = jnp.concatenate([kr[:, half:], kr[:, :half]], axis=1)
    pad = jnp.zeros((w.shape[0], PAIR_W - 2 * B_ROPE - HEADS, w.shape[1]), bf16)
    lat = jnp.concatenate([cq_ckv, kr, kr_sw, cf, pad], axis=1)
    return a, c, g, lat


def _arrange_w_uq(w):
    lead = w.shape[:2]
    w = w.reshape(lead + (PAIRS, 2, B_NOPE + B_ROPE))
    nope = w[..., :B_NOPE].reshape(lead + (PAIRS, 2 * B_NOPE))
    pe = w[..., B_NOPE:]
    half = B_ROPE // 2
    pe_sw = jnp.concatenate([pe[..., half:], pe[..., :half]], axis=-1)
    pe = pe.reshape(lead + (PAIRS, 2 * B_ROPE))
    pe_sw = pe_sw.reshape(lead + (PAIRS, 2 * B_ROPE))
    return jnp.concatenate([nope, pe, pe_sw], axis=-1).reshape(lead + (PAIRS * B_PAIR_W,)).astype(bf16)


def _arrange_w_ukv(w):
    lead = w.shape[:2]
    w = w.reshape(lead + (HEADS, B_NOPE + HEAD_DIM))
    wk = w[..., :B_NOPE].reshape(lead + (BRANCH_W,))
    wv = w[..., B_NOPE:].reshape(lead + (BRANCH_W,))
    return wk.astype(bf16), wv.astype(bf16)


def _rel_bias_rows(rel_bias):
    lead = rel_bias.shape[:2]
    n_hi = A_WINDOW - ATT_T
    lo = jnp.broadcast_to(rel_bias[..., :1], lead + (REL_ROW - n_hi - rel_bias.shape[-1],))
    hi = jnp.broadcast_to(rel_bias[..., -1:], lead + (n_hi,))
    rows = jnp.concatenate([hi, rel_bias[..., ::-1], lo], axis=-1).astype(f32)
    return rows.reshape(lead + (1, REL_ROW))


def kernel(x, mem, positions, ln_mix_g, ln_mix_b, w_in, b_gate, b_forget, a_rel_bias, b_q_norm, b_kv_norm, b_w_uq, b_w_ukv, w_branch, w_mix_out, ln_xa_g, ln_xa_b, xa_w_q, xa_w_kv, xa_w_o, ln_ffn_g, ln_ffn_b, ffn_w_gu, ffn_w_down):
    xf = x.reshape(TOKENS, D_MODEL)
    memf = mem.reshape(BATCH * MEM_LEN, D_MODEL)
    pos = positions.reshape(TOKENS, 1)

    half = B_ROPE // 2
    inv_freq = ROPE_BASE ** (-jnp.arange(half, dtype=f32) / half)
    dead = jnp.zeros((PAIR_W - 2 * B_ROPE,), f32)
    freq_row = jnp.concatenate([jnp.tile(inv_freq, 4), dead]).reshape(1, PAIR_W)
    sign_row = jnp.concatenate([jnp.tile(jnp.concatenate([-jnp.ones((half,), f32), jnp.ones((half,), f32)]), 2),
                                dead]).reshape(1, PAIR_W)
    cos_t, sin_t = _rope_tables(pos, freq_row, sign_row)

    vec = lambda p: p.reshape(DEPTH, 1, p.shape[-1])
    wa, wc, wg, wl = _arrange_w_in(w_in)
    wq = _arrange_w_uq(b_w_uq)
    wk, wv = _arrange_w_ukv(b_w_ukv)
    rel_rows = _rel_bias_rows(a_rel_bias)
    w_branch_b, w_out_b = w_branch.astype(bf16), w_mix_out.astype(bf16)
    xa_q, xa_kv, xa_o = xa_w_q.astype(bf16), xa_w_kv.astype(bf16), xa_w_o.astype(bf16)
    w_gu, w_down = ffn_w_gu.astype(bf16), ffn_w_down.astype(bf16)
    bg = b_gate.reshape(DEPTH, 1, 3 * D_MODEL)
    bf = b_forget.reshape(DEPTH, HEADS, 1)

    for l in range(DEPTH):
        qkv_a, qkv_c, cf_t, q_b, k_b, v_b = _inproj(
            xf, wa, wc, wl, cos_t, sin_t, vec(b_q_norm), vec(b_kv_norm), wq, wk, wv, l)
        forget = _forget_cumsum(cf_t, bf, l)
        rel_table = _rel_table(rel_rows, l)

        y_a = _attention(qkv_a, qkv_a, qkv_a, rel_table, mode="rel", dq=PAIR_W, tq=ATT_T, pps=4,
                         kcol0=PAIRS, vcol0=2 * PAIRS)
        y_b = _attention(q_b, k_b, v_b, None, mode="mla", dq=B_PAIR_W, tq=CAUSAL_T, pps=2, kcol0=0, vcol0=0)
        y_c = _attention(qkv_c, qkv_c, qkv_c, forget, mode="fox", dq=PAIR_W, tq=CAUSAL_T, pps=2,
                         kcol0=PAIRS, vcol0=2 * PAIRS)

        xf = _merge(xf, y_a, y_b, y_c, wg, bg, w_branch_b, w_out_b, vec(ln_mix_g), vec(ln_mix_b), l)
        xf = _xattn(xf, memf, xa_q, xa_kv, xa_o, vec(ln_xa_g), vec(ln_xa_b), l)
        xf = _ffn(xf, w_gu, w_down, vec(ln_ffn_g), vec(ln_ffn_b), l)
    return xf.reshape(BATCH, SEQ, D_MODEL)
```
